```python
import jax, jax.numpy as jnp
from jax import lax
import numpy as np

D_MODEL = 2048
BATCH = 4
SEQ = 2048
DEPTH = 1

MEM_LEN = 256
HEAD_DIM = 128
POOL_WINDOWS = (2, 4, 8, 16)
POOL_WIDTH = D_MODEL // 4
POOL_GROUP = POOL_WIDTH // len(POOL_WINDOWS)
ATTN_WIDTH = D_MODEL - POOL_WIDTH
DILATED_PAIRS = ((128, 1), (512, 4), (2048, 16))
ATTN_HEADS = ATTN_WIDTH // HEAD_DIM
HEADS_PER_GROUP = ATTN_HEADS // len(DILATED_PAIRS)
ATTN_OUT_WIDTH = HEADS_PER_GROUP * HEAD_DIM
IN_PROJ_WIDTH = POOL_WIDTH + 3 * ATTN_WIDTH
MIX_OUT_WIDTH = POOL_WIDTH + ATTN_OUT_WIDTH
CROSS_HEADS = 4
CROSS_HEAD_DIM = D_MODEL // CROSS_HEADS
D_FF = 11 * D_MODEL // 4
ROPE_THETA = 10000.0
EPS = 1e-6
BAND_BLOCK = 64
NEG = -1e30

kernel_name = "hybrid_pool_dilated_attn_macaron_block"


def rmsnorm(x, g):
    xf = x.astype(jnp.float32)
    y = xf * lax.rsqrt(jnp.mean(xf * xf, axis=-1, keepdims=True) + EPS)
    return (y * g.astype(jnp.float32)).astype(x.dtype)


def swiglu(x, w_gate, w_up, w_down):
    return (jax.nn.silu(x @ w_gate) * (x @ w_up)) @ w_down


def rope(t, positions):
    half = t.shape[-1] // 2
    inv = ROPE_THETA ** (-jnp.arange(half, dtype=jnp.float32) / half)
    ang = positions.astype(jnp.float32)[..., None] * inv
    cos = jnp.cos(ang)[:, :, None, :]
    sin = jnp.sin(ang)[:, :, None, :]
    t1 = t[..., :half].astype(jnp.float32)
    t2 = t[..., half:].astype(jnp.float32)
    return jnp.concatenate([t1 * cos - t2 * sin, t2 * cos + t1 * sin], axis=-1).astype(t.dtype)


def multiscale_pool(u, pool_w, pool_scale):
    B, S, C = u.shape
    uf = u.astype(jnp.float32)
    cs = jnp.concatenate([jnp.zeros((B, 1, C), jnp.float32), jnp.cumsum(uf, axis=1)], axis=1)
    t = jnp.arange(S)
    means = []
    for gi, w in enumerate(POOL_WINDOWS):
        lo = jnp.clip(t - w // 2, 0, S)
        hi = jnp.clip(t + w - w // 2, 0, S)
        seg = cs[:, :, gi * POOL_GROUP:(gi + 1) * POOL_GROUP]
        cnt = (hi - lo).astype(jnp.float32)[None, :, None]
        means.append((seg[:, hi] - seg[:, lo]) / cnt)
    pooled = (jnp.concatenate(means, axis=-1) - uf).reshape(B, S, len(POOL_WINDOWS), POOL_GROUP)
    y = jnp.einsum('bsgc,gcd->bsgd', pooled, pool_w.astype(jnp.float32)).reshape(B, S, C)
    return (y * pool_scale.astype(jnp.float32)).astype(u.dtype)


def dilated_window_attention(q, k, v, window, dilation):
    B, S, H, Dh = q.shape
    L = S // dilation
    W = (window // 2) // dilation
    nb = -(-L // BAND_BLOCK)
    Lp = nb * BAND_BLOCK

    def to_sub(t):
        return t.reshape(B, L, dilation, H, Dh).transpose(0, 2, 1, 3, 4)

    qb = jnp.pad(to_sub(q), ((0, 0), (0, 0), (0, Lp - L), (0, 0), (0, 0)))
    qb = qb.reshape(B, dilation, nb, BAND_BLOCK, H, Dh)

    def key_blocks(t):
        tp = jnp.pad(to_sub(t), ((0, 0), (0, 0), (BAND_BLOCK, Lp - L + BAND_BLOCK), (0, 0), (0, 0)))
        return jnp.concatenate(
            [tp[:, :, i * BAND_BLOCK:i * BAND_BLOCK + Lp].reshape(B, dilation, nb, BAND_BLOCK, H, Dh)
             for i in range(3)], axis=3)

    kb = key_blocks(k)
    vb = key_blocks(v)
    scores = jnp.einsum('brnqhc,brnkhc->brnhqk', qb, kb,
                        preferred_element_type=jnp.float32) * (Dh ** -0.5)
    qi = jnp.arange(BAND_BLOCK)[:, None]
    kj = jnp.arange(3 * BAND_BLOCK)[None, :]
    band = jnp.abs(kj - BAND_BLOCK - qi) <= W
    kpos = jnp.arange(nb)[:, None] * BAND_BLOCK + jnp.arange(3 * BAND_BLOCK)[None, :] - BAND_BLOCK
    valid = band[None] & ((kpos >= 0) & (kpos < L))[:, None, :]
    scores = jnp.where(valid[None, None, :, None], scores, NEG)
    lse = jax.nn.logsumexp(scores, axis=-1)
    p = jnp.exp(scores - lse[..., None])
    out = jnp.einsum('brnhqk,brnkhc->brnqhc', p.astype(v.dtype), vb)
    out = out.reshape(B, dilation, Lp, H, Dh)[:, :, :L].transpose(0, 2, 1, 3, 4).reshape(B, S, H, Dh)
    lse = lse.transpose(0, 1, 2, 4, 3).reshape(B, dilation, Lp, H)[:, :, :L]
    lse = lse.transpose(0, 2, 1, 3).reshape(B, S, H)
    return out, lse


def hybrid_mixer(u, positions, w_in, pool_w, pool_scale, w_out):
    B, S, _ = u.shape
    z = u @ w_in
    u_pool = z[..., :POOL_WIDTH]
    q = z[..., POOL_WIDTH:POOL_WIDTH + ATTN_WIDTH].reshape(B, S, ATTN_HEADS, HEAD_DIM)
    k = z[..., POOL_WIDTH + ATTN_WIDTH:POOL_WIDTH + 2 * ATTN_WIDTH].reshape(B, S, ATTN_HEADS, HEAD_DIM)
    v = z[..., POOL_WIDTH + 2 * ATTN_WIDTH:].reshape(B, S, ATTN_HEADS, HEAD_DIM)
    q = rope(q, positions)
    k = rope(k, positions)
    outs, lses = [], []
    for g, (window, dil) in enumerate(DILATED_PAIRS):
        sl = slice(g * HEADS_PER_GROUP, (g + 1) * HEADS_PER_GROUP)
        o, l = dilated_window_attention(q[:, :, sl], k[:, :, sl], v[:, :, sl], window, dil)
        outs.append(o)
        lses.append(l)
    outs = jnp.stack(outs, axis=0).astype(jnp.float32)
    wts = jax.nn.softmax(jnp.stack(lses, axis=0), axis=0)
    attn = jnp.sum(wts[..., None] * outs, axis=0).reshape(B, S, ATTN_OUT_WIDTH).astype(u.dtype)
    pool = multiscale_pool(u_pool, pool_w, pool_scale)
    return jnp.concatenate([pool, attn], axis=-1) @ w_out


def memory_cross_attention(u, m, w_cq, w_ck, w_cv, w_co):
    B, S, _ = u.shape
    M = m.shape[1]
    q = (u @ w_cq).reshape(B, S, CROSS_HEADS, CROSS_HEAD_DIM)
    k = (m @ w_ck).reshape(B, M, CROSS_HEADS, CROSS_HEAD_DIM)
    v = (m @ w_cv).reshape(B, M, CROSS_HEADS, CROSS_HEAD_DIM)
    s = jnp.einsum('bshc,bmhc->bhsm', q, k, preferred_element_type=jnp.float32) * (CROSS_HEAD_DIM ** -0.5)
    p = jax.nn.softmax(s, axis=-1)
    o = jnp.einsum('bhsm,bmhc->bshc', p.astype(v.dtype), v).reshape(B, S, CROSS_HEADS * CROSS_HEAD_DIM)
    return o @ w_co


def setup_inputs(seed: int = 0) -> dict:
    key = jax.random.key(seed)
    ks = jax.random.split(key, 24)

    def w(k, shape, fan_in):
        return jax.random.normal(k, shape, jnp.float32) * (fan_in ** -0.5)

    def gain(k, shape):
        return 1.0 + 0.05 * jax.random.normal(k, shape, jnp.float32)

    x = jax.random.normal(ks[0], (BATCH, SEQ, D_MODEL), jnp.float32)
    mem = jax.random.normal(ks[1], (BATCH, MEM_LEN, D_MODEL), jnp.float32)
    offs = jax.random.randint(ks[2], (BATCH, 1), 0, 1024, dtype=jnp.int32)
    positions = (jnp.arange(SEQ, dtype=jnp.int32)[None, :] + offs).astype(jnp.int32)
    L = DEPTH
    return {
        "x": x,
        "mem": mem,
        "positions": positions,
        "ffn1_norm": gain(ks[3], (L, D_MODEL)),
        "ffn1_w_gate": w(ks[4], (L, D_MODEL, D_FF), D_MODEL),
        "ffn1_w_up": w(ks[5], (L, D_MODEL, D_FF), D_MODEL),
        "ffn1_w_down": w(ks[6], (L, D_FF, D_MODEL), D_FF),
        "mix_norm": gain(ks[7], (L, D_MODEL)),
        "w_in": w(ks[8], (L, D_MODEL, IN_PROJ_WIDTH), D_MODEL),
        "pool_w": w(ks[9], (L, len(POOL_WINDOWS), POOL_GROUP, POOL_GROUP), POOL_GROUP),
        "pool_scale": gain(ks[10], (L, POOL_WIDTH)),
        "w_out": w(ks[11], (L, MIX_OUT_WIDTH, D_MODEL), MIX_OUT_WIDTH),
        "cross_norm": gain(ks[12], (L, D_MODEL)),
        "mem_norm": gain(ks[13], (L, D_MODEL)),
        "w_cq": w(ks[14], (L, D_MODEL, D_MODEL), D_MODEL),
        "w_ck": w(ks[15], (L, D_MODEL, D_MODEL), D_MODEL),
        "w_cv": w(ks[16], (L, D_MODEL, D_MODEL), D_MODEL),
        "w_co": w(ks[17], (L, D_MODEL, D_MODEL), D_MODEL),
        "ffn2_norm": gain(ks[18], (L, D_MODEL)),
        "ffn2_w_gate": w(ks[19], (L, D_MODEL, D_FF), D_MODEL),
        "ffn2_w_up": w(ks[20], (L, D_MODEL, D_FF), D_MODEL),
        "ffn2_w_down": w(ks[21], (L, D_FF, D_MODEL), D_FF),
        "final_norm": gain(ks[22], (D_MODEL,)),
    }


def reference(x, mem, positions, ffn1_norm, ffn1_w_gate, ffn1_w_up, ffn1_w_down,
              mix_norm, w_in, pool_w, pool_scale, w_out,
              cross_norm, mem_norm, w_cq, w_ck, w_cv, w_co,
              ffn2_norm, ffn2_w_gate, ffn2_w_up, ffn2_w_down, final_norm):
    h = x
    for layer in range(DEPTH):
        h = h + 0.5 * swiglu(rmsnorm(h, ffn1_norm[layer]),
                             ffn1_w_gate[layer], ffn1_w_up[layer], ffn1_w_down[layer])
        h = h + hybrid_mixer(rmsnorm(h, mix_norm[layer]), positions,
                             w_in[layer], pool_w[layer], pool_scale[layer], w_out[layer])
        h = h + memory_cross_attention(rmsnorm(h, cross_norm[layer]), rmsnorm(mem, mem_norm[layer]),
                                       w_cq[layer], w_ck[layer], w_cv[layer], w_co[layer])
        h = h + 0.5 * swiglu(rmsnorm(h, ffn2_norm[layer]),
                             ffn2_w_gate[layer], ffn2_w_up[layer], ffn2_w_down[layer])
    return rmsnorm(h, final_norm)
```

```python
import functools

import jax
import jax.numpy as jnp
from jax import lax
from jax.experimental import pallas as pl
from jax.experimental.pallas import tpu as pltpu

F32 = jnp.float32
BF16 = jnp.bfloat16

D_MODEL = 2048
SEQ = 2048
MEM_LEN = 256
HEAD_DIM = 128
POOL_WINDOWS = (2, 4, 8, 16)
POOL_WIDTH = 512
POOL_GROUP = 128
ATTN_WIDTH = 1536
DILATED_PAIRS = ((128, 1), (512, 4), (2048, 16))
HEADS_PER_GROUP = 4
IN_PROJ_WIDTH = 5120
CROSS_HEADS = 4
CROSS_HEAD_DIM = 512
D_FF = 5632
ROPE_THETA = 10000.0
EPS = 1e-6
NEG = -1e30

VMEM_LIMIT_BYTES = 56 * 1024 * 1024

TOKEN_BLOCK = 512
CROSS_BLOCK = 256
FF_BLOCK = 512
PROJ_BLOCK = 512
POOL_PAD = 32
Q_BLOCK = 128
BAND = 64


def _rms(x, g):
    ms = jnp.mean(x * x, axis=-1, keepdims=True)
    return x * lax.rsqrt(ms + EPS) * g


def _silu(x):
    return x / (1.0 + jnp.exp(-x))


def _ffn_kernel(x_ref, g_ref, wg_ref, wu_ref, wd_ref, fg_ref, o_ref, n_ref, *, final_norm):
    j = pl.program_id(1)

    @pl.when(j == 0)
    def _():
        n_ref[...] = _rms(x_ref[...], g_ref[...]).astype(BF16)

    n = n_ref[...]
    gate = jnp.dot(n, wg_ref[...], preferred_element_type=F32)
    up = jnp.dot(n, wu_ref[...], preferred_element_type=F32)
    a = (_silu(gate) * up).astype(BF16)
    d = jnp.dot(a, wd_ref[...], preferred_element_type=F32)

    @pl.when(j == 0)
    def _():
        o_ref[...] = d

    @pl.when(j > 0)
    def _():
        o_ref[...] += d

    @pl.when(j == pl.num_programs(1) - 1)
    def _():
        h = x_ref[...] + 0.5 * o_ref[...]
        if final_norm:
            h = _rms(h, fg_ref[...])
        o_ref[...] = h


def _ffn(x, g, wg, wu, wd, fg, *, final_norm):
    n_tok = x.shape[0]
    grid = (n_tok // TOKEN_BLOCK, D_FF // FF_BLOCK)
    return pl.pallas_call(
        functools.partial(_ffn_kernel, final_norm=final_norm),
        out_shape=jax.ShapeDtypeStruct((n_tok, D_MODEL), F32),
        grid=grid,
        in_specs=[
            pl.BlockSpec((TOKEN_BLOCK, D_MODEL), lambda i, j: (i, 0)),
            pl.BlockSpec((1, D_MODEL), lambda i, j: (0, 0)),
            pl.BlockSpec((D_MODEL, FF_BLOCK), lambda i, j: (0, j)),
            pl.BlockSpec((D_MODEL, FF_BLOCK), lambda i, j: (0, j)),
            pl.BlockSpec((FF_BLOCK, D_MODEL), lambda i, j: (j, 0)),
            pl.BlockSpec((1, D_MODEL), lambda i, j: (0, 0)),
        ],
        out_specs=pl.BlockSpec((TOKEN_BLOCK, D_MODEL), lambda i, j: (i, 0)),
        scratch_shapes=[pltpu.VMEM((TOKEN_BLOCK, D_MODEL), BF16)],
        compiler_params=pltpu.CompilerParams(
            dimension_semantics=("parallel", "arbitrary"),
            vmem_limit_bytes=VMEM_LIMIT_BYTES),
        name="ffn_final" if final_norm else "ffn",
    )(x, g, wg, wu, wd, fg)


def _in_proj_kernel(h_ref, g_ref, pos_ref, rope_ref, w_ref, o_ref, n_ref, cos_ref, sin_ref):
    j = pl.program_id(1)

    @pl.when(j == 0)
    def _():
        n_ref[...] = _rms(h_ref[...], g_ref[...]).astype(BF16)
        ang = pos_ref[...].astype(F32) * rope_ref[0:1, :]
        cos_ref[...] = jnp.cos(ang)
        sin_ref[...] = jnp.sin(ang) * rope_ref[1:2, :]

    z = jnp.dot(n_ref[...], w_ref[...], preferred_element_type=F32)
    is_rot = jnp.logical_and(j >= 1, j <= 6)

    @pl.when(is_rot)
    def _():
        cos = cos_ref[...]
        sin = sin_ref[...]
        for hd in range(PROJ_BLOCK // HEAD_DIM):
            t = z[:, hd * HEAD_DIM:(hd + 1) * HEAD_DIM]
            o_ref[:, hd * HEAD_DIM:(hd + 1) * HEAD_DIM] = (
                t * cos + pltpu.roll(t, HEAD_DIM // 2, 1) * sin)

    @pl.when(jnp.logical_not(is_rot))
    def _():
        o_ref[...] = z


def _in_proj(h, g, pos, rope_tab, w_in):
    n_tok = h.shape[0]
    grid = (n_tok // TOKEN_BLOCK, IN_PROJ_WIDTH // PROJ_BLOCK)
    return pl.pallas_call(
        _in_proj_kernel,
        out_shape=jax.ShapeDtypeStruct((n_tok, IN_PROJ_WIDTH), F32),
        grid=grid,
        in_specs=[
            pl.BlockSpec((TOKEN_BLOCK, D_MODEL), lambda i, j: (i, 0)),
            pl.BlockSpec((1, D_MODEL), lambda i, j: (0, 0)),
            pl.BlockSpec((TOKEN_BLOCK, 1), lambda i, j: (i, 0)),
            pl.BlockSpec((8, HEAD_DIM), lambda i, j: (0, 0)),
            pl.BlockSpec((D_MODEL, PROJ_BLOCK), lambda i, j: (0, j)),
        ],
        out_specs=pl.BlockSpec((TOKEN_BLOCK, PROJ_BLOCK), lambda i, j: (i, j)),
        scratch_shapes=[
            pltpu.VMEM((TOKEN_BLOCK, D_MODEL), BF16),
            pltpu.VMEM((TOKEN_BLOCK, HEAD_DIM), F32),
            pltpu.VMEM((TOKEN_BLOCK, HEAD_DIM), F32),
        ],
        compiler_params=pltpu.CompilerParams(
            dimension_semantics=("parallel", "arbitrary"),
            vmem_limit_bytes=VMEM_LIMIT_BYTES),
        name="in_proj",
    )(h, g, pos, rope_tab, w_in)


def _pool_kernel(u_ref, w_ref, s_ref, o_ref, a_ref, b_ref):
    S = SEQ
    P = POOL_PAD
    ext = S + 2 * (P // 2)
    lo = P // 2
    zeros = jnp.zeros((P, POOL_GROUP), F32)
    t = lax.broadcasted_iota(jnp.int32, (S, 1), 0)
    for gi, w in enumerate(POOL_WINDOWS):
        u = u_ref[:, gi * POOL_GROUP:(gi + 1) * POOL_GROUP]
        src, dst = a_ref, b_ref
        src[0:P, :] = zeros
        src[P + S:2 * P + S, :] = zeros
        dst[0:P, :] = zeros
        dst[P + S:2 * P + S, :] = zeros
        src[P:P + S, :] = u
        dst[lo:lo + ext, :] = src[lo - 1:lo - 1 + ext, :] + src[lo:lo + ext, :]
        src, dst = dst, src
        k = 2
        while k < w:
            sh = k // 2
            dst[lo:lo + ext, :] = (src[lo - sh:lo - sh + ext, :] + src[lo + sh:lo + sh + ext, :])
            src, dst = dst, src
            k *= 2
        win = src[P:P + S, :]
        cnt = (jnp.minimum(t + (w - w // 2), S) - jnp.maximum(t - w // 2, 0)).astype(F32)
        pooled = win / cnt - u
        y = jnp.dot(pooled.astype(BF16), w_ref[gi].astype(BF16), preferred_element_type=F32)
        o_ref[:, gi * POOL_GROUP:(gi + 1) * POOL_GROUP] = (
            y * s_ref[:, gi * POOL_GROUP:(gi + 1) * POOL_GROUP]).astype(o_ref.dtype)


def _pool(z, pool_w, pool_scale, batch):
    return pl.pallas_call(
        _pool_kernel,
        out_shape=jax.ShapeDtypeStruct((batch * SEQ, POOL_WIDTH), BF16),
        grid=(batch,),
        in_specs=[
            pl.BlockSpec((SEQ, POOL_WIDTH), lambda b: (b, 0)),
            pl.BlockSpec((len(POOL_WINDOWS), POOL_GROUP, POOL_GROUP), lambda b: (0, 0, 0)),
            pl.BlockSpec((1, POOL_WIDTH), lambda b: (0, 0)),
        ],
        out_specs=pl.BlockSpec((SEQ, POOL_WIDTH), lambda b: (b, 0)),
        scratch_shapes=[
            pltpu.VMEM((SEQ + 2 * POOL_PAD, POOL_GROUP), F32),
            pltpu.VMEM((SEQ + 2 * POOL_PAD, POOL_GROUP), F32),
        ],
        compiler_params=pltpu.CompilerParams(
            dimension_semantics=("parallel",),
            vmem_limit_bytes=VMEM_LIMIT_BYTES),
        name="pool",
    )(z, pool_w, pool_scale)


def _band_block(q, k, v, q_off):
    nk = k.shape[0]
    s = lax.dot_general(q.astype(BF16), k.astype(BF16), (((1,), (1,)), ((), ())),
                        preferred_element_type=F32) * (HEAD_DIM ** -0.5)
    qi = lax.broadcasted_iota(jnp.int32, (Q_BLOCK, nk), 0) + q_off
    kj = lax.broadcasted_iota(jnp.int32, (Q_BLOCK, nk), 1)
    s = jnp.where(jnp.abs(kj - qi) <= BAND, s, NEG)
    m = jnp.max(s, axis=-1, keepdims=True)
    p = jnp.exp(s - m)
    l = jnp.sum(p, axis=-1, keepdims=True)
    acc = jnp.dot(p.astype(BF16), v.astype(BF16), preferred_element_type=F32)
    return m, l, acc


def _dil_attn_kernel(q0_ref, q1_ref, q2_ref, k0_ref, k1_ref, k2_ref, v0_ref, v1_ref, v2_ref,
                     o_ref, m_ref, l_ref, acc_ref):
    q_refs = (q0_ref, q1_ref, q2_ref)
    k_refs = (k0_ref, k1_ref, k2_ref)
    v_refs = (v0_ref, v1_ref, v2_ref)

    def put(g, rows, m, l, acc):
        m_ref[g, rows, :] = jnp.broadcast_to(m, (Q_BLOCK, HEAD_DIM))
        l_ref[g, rows, :] = jnp.broadcast_to(l, (Q_BLOCK, HEAD_DIM))
        acc_ref[g, rows, :] = acc

    nqb = SEQ // Q_BLOCK

    def body0(qb, carry):
        q_start = pl.multiple_of(qb * Q_BLOCK, Q_BLOCK)
        k_start = pl.multiple_of(jnp.clip(q_start - BAND, 0, SEQ - 2 * Q_BLOCK), BAND)
        q = q_refs[0][pl.ds(q_start, Q_BLOCK), :]
        k = k_refs[0][pl.ds(k_start, 2 * Q_BLOCK), :]
        v = v_refs[0][pl.ds(k_start, 2 * Q_BLOCK), :]
        m, l, acc = _band_block(q, k, v, q_start - k_start)
        put(0, pl.ds(q_start, Q_BLOCK), m, l, acc)
        return carry

    lax.fori_loop(0, nqb, body0, 0)

    for g in (1, 2):
        dil = DILATED_PAIRS[g][1]
        sub_len = SEQ // dil
        nk = min(2 * Q_BLOCK, sub_len)
        for r in range(dil):
            for qb in range(sub_len // Q_BLOCK):
                q_start = qb * Q_BLOCK
                k_start = min(max(q_start - BAND, 0), sub_len - nk)
                q_rows = pl.ds(r + dil * q_start, Q_BLOCK, stride=dil)
                k_rows = pl.ds(r + dil * k_start, nk, stride=dil)
                m, l, acc = _band_block(q_refs[g][q_rows, :], k_refs[g][k_rows, :],
                                        v_refs[g][k_rows, :], q_start - k_start)
                put(g, q_rows, m, l, acc)

    chunk = 256

    def merge(c, carry):
        rows = pl.ds(pl.multiple_of(c * chunk, chunk), chunk)
        m0, m1, m2 = m_ref[0, rows, :], m_ref[1, rows, :], m_ref[2, rows, :]
        mm = jnp.maximum(jnp.maximum(m0, m1), m2)
        w0, w1, w2 = jnp.exp(m0 - mm), jnp.exp(m1 - mm), jnp.exp(m2 - mm)
        num = w0 * acc_ref[0, rows, :] + w1 * acc_ref[1, rows, :] + w2 * acc_ref[2, rows, :]
        den = w0 * l_ref[0, rows, :] + w1 * l_ref[1, rows, :] + w2 * l_ref[2, rows, :]
        o_ref[rows, :] = (num / den).astype(o_ref.dtype)
        return carry

    lax.fori_loop(0, SEQ // chunk, merge, 0)


def _dil_attn(z, batch):
    q_base = POOL_WIDTH // HEAD_DIM
    k_base = q_base + ATTN_WIDTH // HEAD_DIM
    v_base = k_base + ATTN_WIDTH // HEAD_DIM

    def spec(base, g):
        return pl.BlockSpec((SEQ, HEAD_DIM), lambda b, j: (b, base + g * HEADS_PER_GROUP + j))

    in_specs = ([spec(q_base, g) for g in range(3)] + [spec(k_base, g) for g in range(3)]
                + [spec(v_base, g) for g in range(3)])
    return pl.pallas_call(
        _dil_attn_kernel,
        out_shape=jax.ShapeDtypeStruct((batch * SEQ, HEADS_PER_GROUP * HEAD_DIM), BF16),
        grid=(batch, HEADS_PER_GROUP),
        in_specs=in_specs,
        out_specs=pl.BlockSpec((SEQ, HEAD_DIM), lambda b, j: (b, j)),
        scratch_shapes=[
            pltpu.VMEM((3, SEQ, HEAD_DIM), F32),
            pltpu.VMEM((3, SEQ, HEAD_DIM), F32),
            pltpu.VMEM((3, SEQ, HEAD_DIM), F32),
        ],
        compiler_params=pltpu.CompilerParams(
            dimension_semantics=("parallel", "parallel"),
            vmem_limit_bytes=VMEM_LIMIT_BYTES),
        name="dil_attn",
    )(*([z] * 9))


def _mem_kv_kernel(m_ref, g_ref, wk_ref, wv_ref, k_ref, v_ref):
    n = _rms(m_ref[...], g_ref[...]).astype(BF16)
    k_ref[...] = jnp.dot(n, wk_ref[...], preferred_element_type=F32).astype(k_ref.dtype)
    v_ref[...] = jnp.dot(n, wv_ref[...], preferred_element_type=F32).astype(v_ref.dtype)


def _mem_kv(mem, g, w_ck, w_cv):
    n_mem = mem.shape[0]
    col = 512
    out = jax.ShapeDtypeStruct((n_mem, D_MODEL), BF16)
    return pl.pallas_call(
        _mem_kv_kernel,
        out_shape=(out, out),
        grid=(D_MODEL // col,),
        in_specs=[
            pl.BlockSpec((n_mem, D_MODEL), lambda j: (0, 0)),
            pl.BlockSpec((1, D_MODEL), lambda j: (0, 0)),
            pl.BlockSpec((D_MODEL, col), lambda j: (0, j)),
            pl.BlockSpec((D_MODEL, col), lambda j: (0, j)),
        ],
        out_specs=(pl.BlockSpec((n_mem, col), lambda j: (0, j)),
                   pl.BlockSpec((n_mem, col), lambda j: (0, j))),
        compiler_params=pltpu.CompilerParams(
            dimension_semantics=("parallel",),
            vmem_limit_bytes=VMEM_LIMIT_BYTES),
        name="mem_kv",
    )(mem, g, w_ck, w_cv)


def _mix_cross_kernel(h_ref, pool_ref, attn_ref, wo_ref, g_ref, wq_ref, k_ref, v_ref, wc_ref,
                      o_ref, ctx_ref):
    h2 = (h_ref[...]
          + jnp.dot(pool_ref[...], wo_ref[0:POOL_WIDTH, :], preferred_element_type=F32)
          + jnp.dot(attn_ref[...], wo_ref[POOL_WIDTH:, :], preferred_element_type=F32))
    u = _rms(h2, g_ref[...]).astype(BF16)
    q = jnp.dot(u, wq_ref[...], preferred_element_type=F32).astype(BF16)
    for hd in range(CROSS_HEADS):
        cols = slice(hd * CROSS_HEAD_DIM, (hd + 1) * CROSS_HEAD_DIM)
        s = lax.dot_general(q[:, cols], k_ref[:, cols], (((1,), (1,)), ((), ())),
                            preferred_element_type=F32) * (CROSS_HEAD_DIM ** -0.5)
        m = jnp.max(s, axis=-1, keepdims=True)
        p = jnp.exp(s - m)
        l = jnp.sum(p, axis=-1, keepdims=True)
        ctx = jnp.dot(p.astype(BF16), v_ref[:, cols], preferred_element_type=F32) / l
        ctx_ref[:, cols] = ctx.astype(BF16)
    o_ref[...] = h2 + jnp.dot(ctx_ref[...], wc_ref[...], preferred_element_type=F32)


def _mix_cross(h, pool, attn, w_out, g, w_cq, k_mem, v_mem, w_co):
    n_tok = h.shape[0]
    blocks_per_batch = SEQ // CROSS_BLOCK
    resident = pl.Buffered(1)
    return pl.pallas_call(
        _mix_cross_kernel,
        out_shape=jax.ShapeDtypeStruct((n_tok, D_MODEL), F32),
        grid=(n_tok // CROSS_BLOCK,),
        in_specs=[
            pl.BlockSpec((CROSS_BLOCK, D_MODEL), lambda i: (i, 0)),
            pl.BlockSpec((CROSS_BLOCK, POOL_WIDTH), lambda i: (i, 0)),
            pl.BlockSpec((CROSS_BLOCK, POOL_WIDTH), lambda i: (i, 0)),
            pl.BlockSpec((2 * POOL_WIDTH, D_MODEL), lambda i: (0, 0), pipeline_mode=resident),
            pl.BlockSpec((1, D_MODEL), lambda i: (0, 0)),
            pl.BlockSpec((D_MODEL, D_MODEL), lambda i: (0, 0), pipeline_mode=resident),
            pl.BlockSpec((MEM_LEN, D_MODEL), lambda i: (i // blocks_per_batch, 0)),
            pl.BlockSpec((MEM_LEN, D_MODEL), lambda i: (i // blocks_per_batch, 0)),
            pl.BlockSpec((D_MODEL, D_MODEL), lambda i: (0, 0), pipeline_mode=resident),
        ],
        out_specs=pl.BlockSpec((CROSS_BLOCK, D_MODEL), lambda i: (i, 0)),
        scratch_shapes=[pltpu.VMEM((CROSS_BLOCK, D_MODEL), BF16)],
        compiler_params=pltpu.CompilerParams(
            dimension_semantics=("parallel",),
            vmem_limit_bytes=VMEM_LIMIT_BYTES),
        name="mix_cross",
    )(h, pool, attn, w_out, g, w_cq, k_mem, v_mem, w_co)


def _rope_table():
    half = HEAD_DIM // 2
    inv = ROPE_THETA ** (-jnp.arange(half, dtype=F32) / half)
    sign = jnp.concatenate([-jnp.ones((half,), F32), jnp.ones((half,), F32)])
    tab = jnp.zeros((8, HEAD_DIM), F32)
    return tab.at[0].set(jnp.concatenate([inv, inv])).at[1].set(sign)


def kernel(x, mem, positions, ffn1_norm, ffn1_w_gate, ffn1_w_up, ffn1_w_down, mix_norm, w_in, pool_w, pool_scale, w_out, cross_norm, mem_norm, w_cq, w_ck, w_cv, w_co, ffn2_norm, ffn2_w_gate, ffn2_w_up, ffn2_w_down, final_norm):
    batch, seq, d = x.shape
    assert (seq, d) == (SEQ, D_MODEL) and ffn1_norm.shape[0] == 1
    n_tok = batch * seq
    bf = lambda w: w[0].astype(BF16)
    row = lambda v: v.reshape(1, -1).astype(F32)

    h = x.reshape(n_tok, d)
    pos = positions.reshape(n_tok, 1).astype(jnp.int32)
    fg = row(final_norm)

    h = _ffn(h, row(ffn1_norm), bf(ffn1_w_gate), bf(ffn1_w_up), bf(ffn1_w_down), fg, final_norm=False)
    z = _in_proj(h, row(mix_norm), pos, _rope_table(), bf(w_in))
    pool = _pool(z, pool_w[0], row(pool_scale), batch)
    attn = _dil_attn(z, batch)
    k_mem, v_mem = _mem_kv(mem.reshape(batch * MEM_LEN, d), row(mem_norm), bf(w_ck), bf(w_cv))
    h = _mix_cross(h, pool, attn, bf(w_out), row(cross_norm), bf(w_cq), k_mem, v_mem, bf(w_co))
    out = _ffn(h, row(ffn2_norm), bf(ffn2_w_gate), bf(ffn2_w_up), bf(ffn2_w_down), fg, final_norm=True)
    return out.reshape(batch, seq, d)
```

```python
import functools

import jax
import jax.numpy as jnp
from jax import lax
from jax.experimental import pallas as pl
from jax.experimental.pallas import tpu as pltpu

F32 = jnp.float32
BF16 = jnp.bfloat16

D_MODEL = 2048
SEQ = 2048
MEM_LEN = 256
HEAD_DIM = 128
POOL_WINDOWS = (2, 4, 8, 16)
POOL_WIDTH = 512
POOL_GROUP = 128
ATTN_WIDTH = 1536
DILATIONS = (1, 4, 16)
HEADS_PER_GROUP = 4
GROUP_WIDTH = HEADS_PER_GROUP * HEAD_DIM
IN_PROJ_WIDTH = 5120
CROSS_HEADS = 4
CROSS_HEAD_DIM = 512
D_FF = 5632
ROPE_THETA = 10000.0
EPS = 1e-6
NEG = -1e30

VMEM_LIMIT_BYTES = 56 * 1024 * 1024

TOKEN_BLOCK = 512
CROSS_BLOCK = 256
FF_BLOCK = 512
POOL_PAD = 32
Q_BLOCK = 128
BAND = 64


def _rms(x, g):
    ms = jnp.mean(x * x, axis=-1, keepdims=True)
    return x * lax.rsqrt(ms + EPS) * g


def _silu(x):
    return x / (1.0 + jnp.exp(-x))


def _ffn_kernel(x_ref, g_ref, wg_ref, wu_ref, wd_ref, fg_ref, o_ref, n_ref, *, final_norm):
    j = pl.program_id(1)

    @pl.when(j == 0)
    def _():
        x = x_ref[...]
        n_ref[...] = _rms(x, g_ref[...]).astype(BF16)
        o_ref[...] = x

    n = n_ref[...]
    gate = jnp.dot(n, wg_ref[...], preferred_element_type=F32)
    up = jnp.dot(n, wu_ref[...], preferred_element_type=F32)
    a = (_silu(gate) * (0.5 * up)).astype(BF16)
    o_ref[...] += jnp.dot(a, wd_ref[...], preferred_element_type=F32)

    if final_norm:
        @pl.when(j == pl.num_programs(1) - 1)
        def _():
            o_ref[...] = _rms(o_ref[...], fg_ref[...])


def _ffn(x, g, wg, wu, wd, fg, *, final_norm):
    n_tok = x.shape[0]
    grid = (n_tok // TOKEN_BLOCK, D_FF // FF_BLOCK)
    return pl.pallas_call(
        functools.partial(_ffn_kernel, final_norm=final_norm),
        out_shape=jax.ShapeDtypeStruct((n_tok, D_MODEL), F32),
        grid=grid,
        in_specs=[
            pl.BlockSpec((TOKEN_BLOCK, D_MODEL), lambda i, j: (i, 0)),
            pl.BlockSpec((1, D_MODEL), lambda i, j: (0, 0)),
            pl.BlockSpec((D_MODEL, FF_BLOCK), lambda i, j: (0, j)),
            pl.BlockSpec((D_MODEL, FF_BLOCK), lambda i, j: (0, j)),
            pl.BlockSpec((FF_BLOCK, D_MODEL), lambda i, j: (j, 0)),
            pl.BlockSpec((1, D_MODEL), lambda i, j: (0, 0)),
        ],
        out_specs=pl.BlockSpec((TOKEN_BLOCK, D_MODEL), lambda i, j: (i, 0)),
        scratch_shapes=[pltpu.VMEM((TOKEN_BLOCK, D_MODEL), BF16)],
        compiler_params=pltpu.CompilerParams(
            dimension_semantics=("parallel", "arbitrary"),
            vmem_limit_bytes=VMEM_LIMIT_BYTES),
        name="ffn_final" if final_norm else "ffn",
    )(x, g, wg, wu, wd, fg)


def _in_proj_kernel(h_ref, g_ref, pos_ref, rope_ref, w_ref,
                    pool_ref, q0_ref, q1_ref, q2_ref, k0_ref, k1_ref, k2_ref, v0_ref, v1_ref, v2_ref,
                    n_ref, z_ref):
    outs = ((q0_ref, q1_ref, q2_ref), (k0_ref, k1_ref, k2_ref), (v0_ref, v1_ref, v2_ref))
    n_ref[...] = _rms(h_ref[...], g_ref[...]).astype(BF16)
    ang = pos_ref[...].astype(F32) * rope_ref[0:1, :]
    cos_k = jnp.cos(ang)
    sin_k = jnp.sin(ang) * rope_ref[1:2, :]
    cos_q = cos_k * (HEAD_DIM ** -0.5)
    sin_q = sin_k * (HEAD_DIM ** -0.5)

    for c in range(IN_PROJ_WIDTH // GROUP_WIDTH):
        z = jnp.dot(n_ref[...], w_ref[:, c * GROUP_WIDTH:(c + 1) * GROUP_WIDTH],
                    preferred_element_type=F32)
        if c == 0:
            pool_ref[...] = z
            continue
        kind, g = divmod(c - 1, len(DILATIONS))
        dil = DILATIONS[g]
        out_ref = outs[kind][g]
        rows = TOKEN_BLOCK // dil
        for hd in range(HEADS_PER_GROUP):
            cols = slice(hd * HEAD_DIM, (hd + 1) * HEAD_DIM)
            t = z[:, cols]
            if kind < 2:
                cos, sin = (cos_q, sin_q) if kind == 0 else (cos_k, sin_k)
                t = t * cos + pltpu.roll(t, HEAD_DIM // 2, 1) * sin
            if dil == 1:
                out_ref[:, cols] = t.astype(BF16)
            else:
                z_ref[hd] = t
                for r in range(dil):
                    out_ref[r, :, cols] = z_ref[hd, pl.ds(r, rows, stride=dil), :].astype(BF16)


def _in_proj(h, g, pos, rope_tab, w_in, batch):
    n_tok = h.shape[0]
    bpb = SEQ // TOKEN_BLOCK
    nat = jax.ShapeDtypeStruct((n_tok, GROUP_WIDTH), BF16)
    nat_spec = pl.BlockSpec((TOKEN_BLOCK, GROUP_WIDTH), lambda i: (i, 0))

    def perm(dil):
        shape = jax.ShapeDtypeStruct((batch, dil, SEQ // dil, GROUP_WIDTH), BF16)
        spec = pl.BlockSpec((None, dil, TOKEN_BLOCK // dil, GROUP_WIDTH),
                            lambda i: (i // bpb, 0, i % bpb, 0))
        return shape, spec

    p1, p2 = perm(DILATIONS[1]), perm(DILATIONS[2])
    qkv_shapes = [nat, p1[0], p2[0]] * 3
    qkv_specs = [nat_spec, p1[1], p2[1]] * 3
    return pl.pallas_call(
        _in_proj_kernel,
        out_shape=[jax.ShapeDtypeStruct((n_tok, POOL_WIDTH), F32)] + qkv_shapes,
        grid=(n_tok // TOKEN_BLOCK,),
        in_specs=[
            pl.BlockSpec((TOKEN_BLOCK, D_MODEL), lambda i: (i, 0)),
            pl.BlockSpec((1, D_MODEL), lambda i: (0, 0)),
            pl.BlockSpec((TOKEN_BLOCK, 1), lambda i: (i, 0)),
            pl.BlockSpec((8, HEAD_DIM), lambda i: (0, 0)),
            pl.BlockSpec((D_MODEL, IN_PROJ_WIDTH), lambda i: (0, 0), pipeline_mode=pl.Buffered(1)),
        ],
        out_specs=[pl.BlockSpec((TOKEN_BLOCK, POOL_WIDTH), lambda i: (i, 0))] + qkv_specs,
        scratch_shapes=[
            pltpu.VMEM((TOKEN_BLOCK, D_MODEL), BF16),
            pltpu.VMEM((HEADS_PER_GROUP, TOKEN_BLOCK, HEAD_DIM), F32),
        ],
        compiler_params=pltpu.CompilerParams(
            dimension_semantics=("parallel",),
            vmem_limit_bytes=VMEM_LIMIT_BYTES),
        name="in_proj",
    )(h, g, pos, rope_tab, w_in)


def _pool_kernel(u_ref, w_ref, s_ref, o_ref, a_ref, b_ref):
    S = SEQ
    P = POOL_PAD
    ext = S + 2 * (P // 2)
    lo = P // 2
    zeros = jnp.zeros((P, POOL_GROUP), F32)
    t = lax.broadcasted_iota(jnp.int32, (S, 1), 0)
    for gi, w in enumerate(POOL_WINDOWS):
        u = u_ref[:, gi * POOL_GROUP:(gi + 1) * POOL_GROUP]
        src, dst = a_ref, b_ref
        src[0:P, :] = zeros
        src[P + S:2 * P + S, :] = zeros
        dst[0:P, :] = zeros
        dst[P + S:2 * P + S, :] = zeros
        src[P:P + S, :] = u
        dst[lo:lo + ext, :] = src[lo - 1:lo - 1 + ext, :] + src[lo:lo + ext, :]
        src, dst = dst, src
        k = 2
        while k < w:
            sh = k // 2
            dst[lo:lo + ext, :] = (src[lo - sh:lo - sh + ext, :] + src[lo + sh:lo + sh + ext, :])
            src, dst = dst, src
            k *= 2
        win = src[P:P + S, :]
        cnt = (jnp.minimum(t + (w - w // 2), S) - jnp.maximum(t - w // 2, 0)).astype(F32)
        pooled = win / cnt - u
        y = jnp.dot(pooled.astype(BF16), w_ref[gi].astype(BF16), preferred_element_type=F32)
        o_ref[:, gi * POOL_GROUP:(gi + 1) * POOL_GROUP] = (
            y * s_ref[:, gi * POOL_GROUP:(gi + 1) * POOL_GROUP]).astype(o_ref.dtype)


def _pool(u_pool, pool_w, pool_scale, batch):
    return pl.pallas_call(
        _pool_kernel,
        out_shape=jax.ShapeDtypeStruct((batch * SEQ, POOL_WIDTH), BF16),
        grid=(batch,),
        in_specs=[
            pl.BlockSpec((SEQ, POOL_WIDTH), lambda b: (b, 0)),
            pl.BlockSpec((len(POOL_WINDOWS), POOL_GROUP, POOL_GROUP), lambda b: (0, 0, 0)),
            pl.BlockSpec((1, POOL_WIDTH), lambda b: (0, 0)),
        ],
        out_specs=pl.BlockSpec((SEQ, POOL_WIDTH), lambda b: (b, 0)),
        scratch_shapes=[
            pltpu.VMEM((SEQ + 2 * POOL_PAD, POOL_GROUP), F32),
            pltpu.VMEM((SEQ + 2 * POOL_PAD, POOL_GROUP), F32),
        ],
        compiler_params=pltpu.CompilerParams(
            dimension_semantics=("parallel",),
            vmem_limit_bytes=VMEM_LIMIT_BYTES),
        name="pool",
    )(u_pool, pool_w, pool_scale)


def _band_block(q, k, v, band):
    nk = k.shape[0]
    s = lax.dot_general(q, k, (((1,), (1,)), ((), ())), preferred_element_type=F32)
    s = jnp.where(band > 0.5, s, NEG)
    m = jnp.max(s, axis=-1, keepdims=True)
    p = jnp.exp(s - m).astype(BF16)
    v_ext = jnp.concatenate([v, jnp.ones((nk, HEAD_DIM), BF16)], axis=1)
    return m, jnp.dot(p, v_ext, preferred_element_type=F32)


def _dil_attn_kernel(q0_ref, q1_ref, q2_ref, k0_ref, k1_ref, k2_ref, v0_ref, v1_ref, v2_ref,
                     o_ref, band_ref, m_ref, l_ref, acc_ref):
    q_refs = (q0_ref, q1_ref, q2_ref)
    k_refs = (k0_ref, k1_ref, k2_ref)
    v_refs = (v0_ref, v1_ref, v2_ref)

    qi = lax.broadcasted_iota(jnp.int32, (Q_BLOCK, 2 * Q_BLOCK), 0)
    kj = lax.broadcasted_iota(jnp.int32, (Q_BLOCK, 2 * Q_BLOCK), 1)
    for n in range(3):
        band_ref[n] = (jnp.abs(kj - qi - n * BAND) <= BAND).astype(F32)

    for g, dil in enumerate(DILATIONS):
        sub_len = SEQ // dil
        nk = min(2 * Q_BLOCK, sub_len)
        for r in range(dil):
            for qb in range(sub_len // Q_BLOCK):
                q_start = qb * Q_BLOCK
                k_start = min(max(q_start - BAND, 0), sub_len - nk)
                if dil == 1:
                    src = lambda ref, start, size: ref[start:start + size, :]
                else:
                    src = lambda ref, start, size: ref[r, start:start + size, :]
                m, pv = _band_block(src(q_refs[g], q_start, Q_BLOCK), src(k_refs[g], k_start, nk),
                                    src(v_refs[g], k_start, nk),
                                    band_ref[(q_start - k_start) // BAND, :, 0:nk])
                rows = pl.ds(r + dil * q_start, Q_BLOCK, stride=dil) if dil > 1 else pl.ds(q_start, Q_BLOCK)
                m_ref[g, rows, :] = jnp.broadcast_to(m, (Q_BLOCK, HEAD_DIM))
                acc_ref[g, rows, :] = pv[:, 0:HEAD_DIM]
                l_ref[g, rows, :] = pv[:, HEAD_DIM:]

    chunk = 256

    def merge(c, carry):
        rows = pl.ds(pl.multiple_of(c * chunk, chunk), chunk)
        m0, m1, m2 = m_ref[0, rows, :], m_ref[1, rows, :], m_ref[2, rows, :]
        mm = jnp.maximum(jnp.maximum(m0, m1), m2)
        w0, w1, w2 = jnp.exp(m0 - mm), jnp.exp(m1 - mm), jnp.exp(m2 - mm)
        num = w0 * acc_ref[0, rows, :] + w1 * acc_ref[1, rows, :] + w2 * acc_ref[2, rows, :]
        den = w0 * l_ref[0, rows, :] + w1 * l_ref[1, rows, :] + w2 * l_ref[2, rows, :]
        o_ref[rows, :] = (num / den).astype(o_ref.dtype)
        return carry

    lax.fori_loop(0, SEQ // chunk, merge, 0)


def _dil_attn(qkv, batch):
    def spec(g):
        dil = DILATIONS[g]
        if dil == 1:
            return pl.BlockSpec((SEQ, HEAD_DIM), lambda b, j: (b, j))
        return pl.BlockSpec((None, dil, SEQ // dil, HEAD_DIM), lambda b, j: (b, 0, 0, j))

    return pl.pallas_call(
        _dil_attn_kernel,
        out_shape=jax.ShapeDtypeStruct((batch * SEQ, GROUP_WIDTH), BF16),
        grid=(batch, HEADS_PER_GROUP),
        in_specs=[spec(g) for g in range(3)] * 3,
        out_specs=pl.BlockSpec((SEQ, HEAD_DIM), lambda b, j: (b, j)),
        scratch_shapes=[
            pltpu.VMEM((3, Q_BLOCK, 2 * Q_BLOCK), F32),
            pltpu.VMEM((3, SEQ, HEAD_DIM), F32),
            pltpu.VMEM((3, SEQ, HEAD_DIM), F32),
            pltpu.VMEM((3, SEQ, HEAD_DIM), F32),
        ],
        compiler_params=pltpu.CompilerParams(
            dimension_semantics=("parallel", "parallel"),
            vmem_limit_bytes=VMEM_LIMIT_BYTES),
        name="dil_attn",
    )(*qkv)


def _mem_kv_kernel(m_ref, g_ref, wk_ref, wv_ref, k_ref, v_ref):
    n = _rms(m_ref[...], g_ref[...]).astype(BF16)
    k_ref[...] = jnp.dot(n, wk_ref[...], preferred_element_type=F32).astype(k_ref.dtype)
    v_ref[...] = jnp.dot(n, wv_ref[...], preferred_element_type=F32).astype(v_ref.dtype)


def _mem_kv(mem, g, w_ck, w_cv):
    n_mem = mem.shape[0]
    col = 512
    out = jax.ShapeDtypeStruct((n_mem, D_MODEL), BF16)
    return pl.pallas_call(
        _mem_kv_kernel,
        out_shape=(out, out),
        grid=(D_MODEL // col,),
        in_specs=[
            pl.BlockSpec((n_mem, D_MODEL), lambda j: (0, 0)),
            pl.BlockSpec((1, D_MODEL), lambda j: (0, 0)),
            pl.BlockSpec((D_MODEL, col), lambda j: (0, j)),
            pl.BlockSpec((D_MODEL, col), lambda j: (0, j)),
        ],
        out_specs=(pl.BlockSpec((n_mem, col), lambda j: (0, j)),
                   pl.BlockSpec((n_mem, col), lambda j: (0, j))),
        compiler_params=pltpu.CompilerParams(
            dimension_semantics=("parallel",),
            vmem_limit_bytes=VMEM_LIMIT_BYTES),
        name="mem_kv",
    )(mem, g, w_ck, w_cv)


def _mix_cross_kernel(h_ref, pool_ref, attn_ref, wo_ref, g_ref, wq_ref, k_ref, v_ref, wc_ref,
                      o_ref, ctx_ref):
    h2 = (h_ref[...]
          + jnp.dot(pool_ref[...], wo_ref[0:POOL_WIDTH, :], preferred_element_type=F32)
          + jnp.dot(attn_ref[...], wo_ref[POOL_WIDTH:, :], preferred_element_type=F32))
    u = _rms(h2, g_ref[...]).astype(BF16)
    q = jnp.dot(u, wq_ref[...], preferred_element_type=F32).astype(BF16)
    for hd in range(CROSS_HEADS):
        cols = slice(hd * CROSS_HEAD_DIM, (hd + 1) * CROSS_HEAD_DIM)
        s = lax.dot_general(q[:, cols], k_ref[:, cols], (((1,), (1,)), ((), ())),
                            preferred_element_type=F32) * (CROSS_HEAD_DIM ** -0.5)
        m = jnp.max(s, axis=-1, keepdims=True)
        p = jnp.exp(s - m)
        l = jnp.sum(p, axis=-1, keepdims=True)
        ctx = jnp.dot(p.astype(BF16), v_ref[:, cols], preferred_element_type=F32) / l
        ctx_ref[:, cols] = ctx.astype(BF16)
    o_ref[...] = h2 + jnp.dot(ctx_ref[...], wc_ref[...], preferred_element_type=F32)


def _mix_cross(h, pool, attn, w_out, g, w_cq, k_mem, v_mem, w_co):
    n_tok = h.shape[0]
    blocks_per_batch = SEQ // CROSS_BLOCK
    resident = pl.Buffered(1)
    return pl.pallas_call(
        _mix_cross_kernel,
        out_shape=jax.ShapeDtypeStruct((n_tok, D_MODEL), F32),
        grid=(n_tok // CROSS_BLOCK,),
        in_specs=[
            pl.BlockSpec((CROSS_BLOCK, D_MODEL), lambda i: (i, 0)),
            pl.BlockSpec((CROSS_BLOCK, POOL_WIDTH), lambda i: (i, 0)),
            pl.BlockSpec((CROSS_BLOCK, POOL_WIDTH), lambda i: (i, 0)),
            pl.BlockSpec((2 * POOL_WIDTH, D_MODEL), lambda i: (0, 0), pipeline_mode=resident),
            pl.BlockSpec((1, D_MODEL), lambda i: (0, 0)),
            pl.BlockSpec((D_MODEL, D_MODEL), lambda i: (0, 0), pipeline_mode=resident),
            pl.BlockSpec((MEM_LEN, D_MODEL), lambda i: (i // blocks_per_batch, 0)),
            pl.BlockSpec((MEM_LEN, D_MODEL), lambda i: (i // blocks_per_batch, 0)),
            pl.BlockSpec((D_MODEL, D_MODEL), lambda i: (0, 0), pipeline_mode=resident),
        ],
        out_specs=pl.BlockSpec((CROSS_BLOCK, D_MODEL), lambda i: (i, 0)),
        scratch_shapes=[pltpu.VMEM((CROSS_BLOCK, D_MODEL), BF16)],
        compiler_params=pltpu.CompilerParams(
            dimension_semantics=("parallel",),
            vmem_limit_bytes=VMEM_LIMIT_BYTES),
        name="mix_cross",
    )(h, pool, attn, w_out, g, w_cq, k_mem, v_mem, w_co)


def _rope_table():
    half = HEAD_DIM // 2
    inv = ROPE_THETA ** (-jnp.arange(half, dtype=F32) / half)
    sign = jnp.concatenate([-jnp.ones((half,), F32), jnp.ones((half,), F32)])
    tab = jnp.zeros((8, HEAD_DIM), F32)
    return tab.at[0].set(jnp.concatenate([inv, inv])).at[1].set(sign)


def kernel(x, mem, positions, ffn1_norm, ffn1_w_gate, ffn1_w_up, ffn1_w_down, mix_norm, w_in, pool_w, pool_scale, w_out, cross_norm, mem_norm, w_cq, w_ck, w_cv, w_co, ffn2_norm, ffn2_w_gate, ffn2_w_up, ffn2_w_down, final_norm):
    batch, seq, d = x.shape
    assert (seq, d) == (SEQ, D_MODEL) and ffn1_norm.shape[0] == 1
    n_tok = batch * seq
    bf = lambda w: w[0].astype(BF16)
    row = lambda v: v.reshape(1, -1).astype(F32)

    h = x.reshape(n_tok, d)
    pos = positions.reshape(n_tok, 1).astype(jnp.int32)
    fg = row(final_norm)

    h = _ffn(h, row(ffn1_norm), bf(ffn1_w_gate), bf(ffn1_w_up), bf(ffn1_w_down), fg, final_norm=False)
    u_pool, *qkv = _in_proj(h, row(mix_norm), pos, _rope_table(), bf(w_in), batch)
    pool = _pool(u_pool, pool_w[0], row(pool_scale), batch)
    attn = _dil_attn(qkv, batch)
    k_mem, v_mem = _mem_kv(mem.reshape(batch * MEM_LEN, d), row(mem_norm), bf(w_ck), bf(w_cv))
    h = _mix_cross(h, pool, attn, bf(w_out), row(cross_norm), bf(w_cq), k_mem, v_mem, bf(w_co))
    out = _ffn(h, row(ffn2_norm), bf(ffn2_w_gate), bf(ffn2_w_up), bf(ffn2_w_down), fg, final_norm=True)
    return out.reshape(batch, seq, d)
```

```python
import functools

import jax
import jax.numpy as jnp
from jax import lax
from jax.experimental import pallas as pl
from jax.experimental.pallas import tpu as pltpu

F32 = jnp.float32
BF16 = jnp.bfloat16

D_MODEL = 2048
SEQ = 2048
MEM_LEN = 256
HEAD_DIM = 128
POOL_WINDOWS = (2, 4, 8, 16)
POOL_WIDTH = 512
POOL_GROUP = 128
ATTN_WIDTH = 1536
DILATIONS = (1, 4, 16)
HEADS_PER_GROUP = 4
GROUP_WIDTH = HEADS_PER_GROUP * HEAD_DIM
IN_PROJ_WIDTH = 5120
CROSS_HEADS = 4
CROSS_HEAD_DIM = 512
D_FF = 5632
ROPE_THETA = 10000.0
EPS = 1e-6
NEG = -1e30

VMEM_LIMIT_BYTES = 56 * 1024 * 1024
FFN_VMEM_LIMIT_BYTES = 60 * 1024 * 1024

TOKEN_BLOCK = 512
FFN_TOKEN_BLOCK = 1024
CROSS_BLOCK = 256
FF_BLOCK = 512
POOL_PAD = 32
Q_BLOCK = 128
BAND = 64


def _rms(x, g):
    ms = jnp.mean(x * x, axis=-1, keepdims=True)
    return x * lax.rsqrt(ms + EPS) * g


def _silu(x):
    return x / (1.0 + jnp.exp(-x))


def _ffn_kernel(x_ref, g_ref, wg_ref, wu_ref, wd_ref, fg_ref, o_ref, n_ref, *, final_norm):
    j = pl.program_id(1)

    @pl.when(j == 0)
    def _():
        x = x_ref[...]
        n_ref[...] = _rms(x, g_ref[...]).astype(BF16)
        o_ref[...] = x

    n = n_ref[...]
    gate = jnp.dot(n, wg_ref[...], preferred_element_type=F32)
    up = jnp.dot(n, wu_ref[...], preferred_element_type=F32)
    a = (_silu(gate) * (0.5 * up)).astype(BF16)
    o_ref[...] += jnp.dot(a, wd_ref[...], preferred_element_type=F32)

    if final_norm:
        @pl.when(j == pl.num_programs(1) - 1)
        def _():
            o_ref[...] = _rms(o_ref[...], fg_ref[...])


def _ffn(x, g, wg, wu, wd, fg, *, final_norm):
    n_tok = x.shape[0]
    grid = (n_tok // FFN_TOKEN_BLOCK, D_FF // FF_BLOCK)
    return pl.pallas_call(
        functools.partial(_ffn_kernel, final_norm=final_norm),
        out_shape=jax.ShapeDtypeStruct((n_tok, D_MODEL), F32),
        grid=grid,
        in_specs=[
            pl.BlockSpec((FFN_TOKEN_BLOCK, D_MODEL), lambda i, j: (i, 0)),
            pl.BlockSpec((1, D_MODEL), lambda i, j: (0, 0)),
            pl.BlockSpec((D_MODEL, FF_BLOCK), lambda i, j: (0, j)),
            pl.BlockSpec((D_MODEL, FF_BLOCK), lambda i, j: (0, j)),
            pl.BlockSpec((FF_BLOCK, D_MODEL), lambda i, j: (j, 0)),
            pl.BlockSpec((1, D_MODEL), lambda i, j: (0, 0)),
        ],
        out_specs=pl.BlockSpec((FFN_TOKEN_BLOCK, D_MODEL), lambda i, j: (i, 0)),
        scratch_shapes=[pltpu.VMEM((FFN_TOKEN_BLOCK, D_MODEL), BF16)],
        compiler_params=pltpu.CompilerParams(
            dimension_semantics=("parallel", "arbitrary"),
            vmem_limit_bytes=FFN_VMEM_LIMIT_BYTES),
        name="ffn_final" if final_norm else "ffn",
    )(x, g, wg, wu, wd, fg)


def _in_proj_kernel(h_ref, g_ref, pos_ref, rope_ref, w_ref,
                    pool_ref, q0_ref, q1_ref, q2_ref, k0_ref, k1_ref, k2_ref, v0_ref, v1_ref, v2_ref,
                    n_ref, z_ref):
    outs = ((q0_ref, q1_ref, q2_ref), (k0_ref, k1_ref, k2_ref), (v0_ref, v1_ref, v2_ref))
    n_ref[...] = _rms(h_ref[...], g_ref[...]).astype(BF16)
    ang = pos_ref[...].astype(F32) * rope_ref[0:1, :]
    cos_k = jnp.cos(ang)
    sin_k = jnp.sin(ang) * rope_ref[1:2, :]
    cos_q = cos_k * (HEAD_DIM ** -0.5)
    sin_q = sin_k * (HEAD_DIM ** -0.5)

    for c in range(IN_PROJ_WIDTH // GROUP_WIDTH):
        z = jnp.dot(n_ref[...], w_ref[:, c * GROUP_WIDTH:(c + 1) * GROUP_WIDTH],
                    preferred_element_type=F32)
        if c == 0:
            pool_ref[...] = z
            continue
        kind, g = divmod(c - 1, len(DILATIONS))
        dil = DILATIONS[g]
        out_ref = outs[kind][g]
        rows = TOKEN_BLOCK // dil
        for hd in range(HEADS_PER_GROUP):
            cols = slice(hd * HEAD_DIM, (hd + 1) * HEAD_DIM)
            t = z[:, cols]
            if kind < 2:
                cos, sin = (cos_q, sin_q) if kind == 0 else (cos_k, sin_k)
                t = t * cos + pltpu.roll(t, HEAD_DIM // 2, 1) * sin
            if dil == 1:
                out_ref[:, cols] = t.astype(BF16)
            else:
                z_ref[hd] = t
                for r in range(dil):
                    out_ref[r, :, cols] = z_ref[hd, pl.ds(r, rows, stride=dil), :].astype(BF16)


def _in_proj(h, g, pos, rope_tab, w_in, batch):
    n_tok = h.shape[0]
    bpb = SEQ // TOKEN_BLOCK
    nat = jax.ShapeDtypeStruct((n_tok, GROUP_WIDTH), BF16)
    nat_spec = pl.BlockSpec((TOKEN_BLOCK, GROUP_WIDTH), lambda i: (i, 0))

    def perm(dil):
        shape = jax.ShapeDtypeStruct((batch, dil, SEQ // dil, GROUP_WIDTH), BF16)
        spec = pl.BlockSpec((None, dil, TOKEN_BLOCK // dil, GROUP_WIDTH),
                            lambda i: (i // bpb, 0, i % bpb, 0))
        return shape, spec

    p1, p2 = perm(DILATIONS[1]), perm(DILATIONS[2])
    qkv_shapes = [nat, p1[0], p2[0]] * 3
    qkv_specs = [nat_spec, p1[1], p2[1]] * 3
    return pl.pallas_call(
        _in_proj_kernel,
        out_shape=[jax.ShapeDtypeStruct((n_tok, POOL_WIDTH), F32)] + qkv_shapes,
        grid=(n_tok // TOKEN_BLOCK,),
        in_specs=[
            pl.BlockSpec((TOKEN_BLOCK, D_MODEL), lambda i: (i, 0)),
            pl.BlockSpec((1, D_MODEL), lambda i: (0, 0)),
            pl.BlockSpec((TOKEN_BLOCK, 1), lambda i: (i, 0)),
            pl.BlockSpec((8, HEAD_DIM), lambda i: (0, 0)),
            pl.BlockSpec((D_MODEL, IN_PROJ_WIDTH), lambda i: (0, 0), pipeline_mode=pl.Buffered(1)),
        ],
        out_specs=[pl.BlockSpec((TOKEN_BLOCK, POOL_WIDTH), lambda i: (i, 0))] + qkv_specs,
        scratch_shapes=[
            pltpu.VMEM((TOKEN_BLOCK, D_MODEL), BF16),
            pltpu.VMEM((HEADS_PER_GROUP, TOKEN_BLOCK, HEAD_DIM), F32),
        ],
        compiler_params=pltpu.CompilerParams(
            dimension_semantics=("parallel",),
            vmem_limit_bytes=VMEM_LIMIT_BYTES),
        name="in_proj",
    )(h, g, pos, rope_tab, w_in)


def _pool_kernel(u_ref, w_ref, s_ref, o_ref, a_ref, b_ref):
    S = SEQ
    P = POOL_PAD
    ext = S + 2 * (P // 2)
    lo = P // 2
    zeros = jnp.zeros((P, POOL_GROUP), F32)
    t = lax.broadcasted_iota(jnp.int32, (S, 1), 0)
    for gi, w in enumerate(POOL_WINDOWS):
        u = u_ref[:, gi * POOL_GROUP:(gi + 1) * POOL_GROUP]
        src, dst = a_ref, b_ref
        src[0:P, :] = zeros
        src[P + S:2 * P + S, :] = zeros
        dst[0:P, :] = zeros
        dst[P + S:2 * P + S, :] = zeros
        src[P:P + S, :] = u
        dst[lo:lo + ext, :] = src[lo - 1:lo - 1 + ext, :] + src[lo:lo + ext, :]
        src, dst = dst, src
        k = 2
        while k < w:
            sh = k // 2
            dst[lo:lo + ext, :] = (src[lo - sh:lo - sh + ext, :] + src[lo + sh:lo + sh + ext, :])
            src, dst = dst, src
            k *= 2
        win = src[P:P + S, :]
        cnt = (jnp.minimum(t + (w - w // 2), S) - jnp.maximum(t - w // 2, 0)).astype(F32)
        pooled = win / cnt - u
        y = jnp.dot(pooled.astype(BF16), w_ref[gi].astype(BF16), preferred_element_type=F32)
        o_ref[:, gi * POOL_GROUP:(gi + 1) * POOL_GROUP] = (
            y * s_ref[:, gi * POOL_GROUP:(gi + 1) * POOL_GROUP]).astype(o_ref.dtype)


def _pool(u_pool, pool_w, pool_scale, batch):
    return pl.pallas_call(
        _pool_kernel,
        out_shape=jax.ShapeDtypeStruct((batch * SEQ, POOL_WIDTH), BF16),
        grid=(batch,),
        in_specs=[
            pl.BlockSpec((SEQ, POOL_WIDTH), lambda b: (b, 0)),
            pl.BlockSpec((len(POOL_WINDOWS), POOL_GROUP, POOL_GROUP), lambda b: (0, 0, 0)),
            pl.BlockSpec((1, POOL_WIDTH), lambda b: (0, 0)),
        ],
        out_specs=pl.BlockSpec((SEQ, POOL_WIDTH), lambda b: (b, 0)),
        scratch_shapes=[
            pltpu.VMEM((SEQ + 2 * POOL_PAD, POOL_GROUP), F32),
            pltpu.VMEM((SEQ + 2 * POOL_PAD, POOL_GROUP), F32),
        ],
        compiler_params=pltpu.CompilerParams(
            dimension_semantics=("parallel",),
            vmem_limit_bytes=VMEM_LIMIT_BYTES),
        name="pool",
    )(u_pool, pool_w, pool_scale)


def _band_block(q, k, v, band):
    nk = k.shape[0]
    s = lax.dot_general(q, k, (((1,), (1,)), ((), ())), preferred_element_type=F32)
    s = jnp.where(band > 0.5, s, NEG)
    m = jnp.max(s, axis=-1, keepdims=True)
    p = jnp.exp(s - m).astype(BF16)
    v_ext = jnp.concatenate([v, jnp.ones((nk, HEAD_DIM), BF16)], axis=1)
    return m, jnp.dot(p, v_ext, preferred_element_type=F32)


def _dil_attn_kernel(q0_ref, q1_ref, q2_ref, k0_ref, k1_ref, k2_ref, v0_ref, v1_ref, v2_ref,
                     o_ref, band_ref, m_ref, l_ref, acc_ref):
    q_refs = (q0_ref, q1_ref, q2_ref)
    k_refs = (k0_ref, k1_ref, k2_ref)
    v_refs = (v0_ref, v1_ref, v2_ref)

    qi = lax.broadcasted_iota(jnp.int32, (Q_BLOCK, 2 * Q_BLOCK), 0)
    kj = lax.broadcasted_iota(jnp.int32, (Q_BLOCK, 2 * Q_BLOCK), 1)
    for n in range(3):
        band_ref[n] = (jnp.abs(kj - qi - n * BAND) <= BAND).astype(F32)

    for g, dil in enumerate(DILATIONS):
        sub_len = SEQ // dil
        nk = min(2 * Q_BLOCK, sub_len)
        for r in range(dil):
            for qb in range(sub_len // Q_BLOCK):
                q_start = qb * Q_BLOCK
                k_start = min(max(q_start - BAND, 0), sub_len - nk)
                if dil == 1:
                    src = lambda ref, start, size: ref[start:start + size, :]
                else:
                    src = lambda ref, start, size: ref[r, start:start + size, :]
                m, pv = _band_block(src(q_refs[g], q_start, Q_BLOCK), src(k_refs[g], k_start, nk),
                                    src(v_refs[g], k_start, nk),
                                    band_ref[(q_start - k_start) // BAND, :, 0:nk])
                rows = pl.ds(r + dil * q_start, Q_BLOCK, stride=dil) if dil > 1 else pl.ds(q_start, Q_BLOCK)
                m_ref[g, rows, :] = jnp.broadcast_to(m, (Q_BLOCK, HEAD_DIM))
                acc_ref[g, rows, :] = pv[:, 0:HEAD_DIM]
                l_ref[g, rows, :] = pv[:, HEAD_DIM:]

    chunk = 256

    def merge(c, carry):
        rows = pl.ds(pl.multiple_of(c * chunk, chunk), chunk)
        m0, m1, m2 = m_ref[0, rows, :], m_ref[1, rows, :], m_ref[2, rows, :]
        mm = jnp.maximum(jnp.maximum(m0, m1), m2)
        w0, w1, w2 = jnp.exp(m0 - mm), jnp.exp(m1 - mm), jnp.exp(m2 - mm)
        num = w0 * acc_ref[0, rows, :] + w1 * acc_ref[1, rows, :] + w2 * acc_ref[2, rows, :]
        den = w0 * l_ref[0, rows, :] + w1 * l_ref[1, rows, :] + w2 * l_ref[2, rows, :]
        o_ref[rows, :] = (num / den).astype(o_ref.dtype)
        return carry

    lax.fori_loop(0, SEQ // chunk, merge, 0)


def _dil_attn(qkv, batch):
    def spec(g):
        dil = DILATIONS[g]
        if dil == 1:
            return pl.BlockSpec((SEQ, HEAD_DIM), lambda b, j: (b, j))
        return pl.BlockSpec((None, dil, SEQ // dil, HEAD_DIM), lambda b, j: (b, 0, 0, j))

    return pl.pallas_call(
        _dil_attn_kernel,
        out_shape=jax.ShapeDtypeStruct((batch * SEQ, GROUP_WIDTH), BF16),
        grid=(batch, HEADS_PER_GROUP),
        in_specs=[spec(g) for g in range(3)] * 3,
        out_specs=pl.BlockSpec((SEQ, HEAD_DIM), lambda b, j: (b, j)),
        scratch_shapes=[
            pltpu.VMEM((3, Q_BLOCK, 2 * Q_BLOCK), F32),
            pltpu.VMEM((3, SEQ, HEAD_DIM), F32),
            pltpu.VMEM((3, SEQ, HEAD_DIM), F32),
            pltpu.VMEM((3, SEQ, HEAD_DIM), F32),
        ],
        compiler_params=pltpu.CompilerParams(
            dimension_semantics=("parallel", "parallel"),
            vmem_limit_bytes=VMEM_LIMIT_BYTES),
        name="dil_attn",
    )(*qkv)


def _mem_kv_kernel(m_ref, g_ref, wk_ref, wv_ref, k_ref, v_ref):
    n = _rms(m_ref[...], g_ref[...]).astype(BF16)
    k_ref[...] = jnp.dot(n, wk_ref[...], preferred_element_type=F32).astype(k_ref.dtype)
    v_ref[...] = jnp.dot(n, wv_ref[...], preferred_element_type=F32).astype(v_ref.dtype)


def _mem_kv(mem, g, w_ck, w_cv):
    n_mem = mem.shape[0]
    col = 512
    out = jax.ShapeDtypeStruct((n_mem, D_MODEL), BF16)
    return pl.pallas_call(
        _mem_kv_kernel,
        out_shape=(out, out),
        grid=(D_MODEL // col,),
        in_specs=[
            pl.BlockSpec((n_mem, D_MODEL), lambda j: (0, 0)),
            pl.BlockSpec((1, D_MODEL), lambda j: (0, 0)),
            pl.BlockSpec((D_MODEL, col), lambda j: (0, j)),
            pl.BlockSpec((D_MODEL, col), lambda j: (0, j)),
        ],
        out_specs=(pl.BlockSpec((n_mem, col), lambda j: (0, j)),
                   pl.BlockSpec((n_mem, col), lambda j: (0, j))),
        compiler_params=pltpu.CompilerParams(
            dimension_semantics=("parallel",),
            vmem_limit_bytes=VMEM_LIMIT_BYTES),
        name="mem_kv",
    )(mem, g, w_ck, w_cv)


def _mix_cross_kernel(h_ref, pool_ref, attn_ref, wo_ref, g_ref, wq_ref, k_ref, v_ref, wc_ref,
                      o_ref, ctx_ref):
    h2 = (h_ref[...]
          + jnp.dot(pool_ref[...], wo_ref[0:POOL_WIDTH, :], preferred_element_type=F32)
          + jnp.dot(attn_ref[...], wo_ref[POOL_WIDTH:, :], preferred_element_type=F32))
    u = _rms(h2, g_ref[...]).astype(BF16)
    q = jnp.dot(u, wq_ref[...], preferred_element_type=F32).astype(BF16)
    for hd in range(CROSS_HEADS):
        cols = slice(hd * CROSS_HEAD_DIM, (hd + 1) * CROSS_HEAD_DIM)
        s = lax.dot_general(q[:, cols], k_ref[:, cols], (((1,), (1,)), ((), ())),
                            preferred_element_type=F32) * (CROSS_HEAD_DIM ** -0.5)
        m = jnp.max(s, axis=-1, keepdims=True)
        p = jnp.exp(s - m)
        l = jnp.sum(p, axis=-1, keepdims=True)
        ctx = jnp.dot(p.astype(BF16), v_ref[:, cols], preferred_element_type=F32) / l
        ctx_ref[:, cols] = ctx.astype(BF16)
    o_ref[...] = h2 + jnp.dot(ctx_ref[...], wc_ref[...], preferred_element_type=F32)


def _mix_cross(h, pool, attn, w_out, g, w_cq, k_mem, v_mem, w_co):
    n_tok = h.shape[0]
    blocks_per_batch = SEQ // CROSS_BLOCK
    resident = pl.Buffered(1)
    return pl.pallas_call(
        _mix_cross_kernel,
        out_shape=jax.ShapeDtypeStruct((n_tok, D_MODEL), F32),
        grid=(n_tok // CROSS_BLOCK,),
        in_specs=[
            pl.BlockSpec((CROSS_BLOCK, D_MODEL), lambda i: (i, 0)),
            pl.BlockSpec((CROSS_BLOCK, POOL_WIDTH), lambda i: (i, 0)),
            pl.BlockSpec((CROSS_BLOCK, POOL_WIDTH), lambda i: (i, 0)),
            pl.BlockSpec((2 * POOL_WIDTH, D_MODEL), lambda i: (0, 0), pipeline_mode=resident),
            pl.BlockSpec((1, D_MODEL), lambda i: (0, 0)),
            pl.BlockSpec((D_MODEL, D_MODEL), lambda i: (0, 0), pipeline_mode=resident),
            pl.BlockSpec((MEM_LEN, D_MODEL), lambda i: (i // blocks_per_batch, 0)),
            pl.BlockSpec((MEM_LEN, D_MODEL), lambda i: (i // blocks_per_batch, 0)),
            pl.BlockSpec((D_MODEL, D_MODEL), lambda i: (0, 0), pipeline_mode=resident),
        ],
        out_specs=pl.BlockSpec((CROSS_BLOCK, D_MODEL), lambda i: (i, 0)),
        scratch_shapes=[pltpu.VMEM((CROSS_BLOCK, D_MODEL), BF16)],
        compiler_params=pltpu.CompilerParams(
            dimension_semantics=("parallel",),
            vmem_limit_bytes=VMEM_LIMIT_BYTES),
        name="mix_cross",
    )(h, pool, attn, w_out, g, w_cq, k_mem, v_mem, w_co)


def _rope_table():
    half = HEAD_DIM // 2
    inv = ROPE_THETA ** (-jnp.arange(half, dtype=F32) / half)
    sign = jnp.concatenate([-jnp.ones((half,), F32), jnp.ones((half,), F32)])
    tab = jnp.zeros((8, HEAD_DIM), F32)
    return tab.at[0].set(jnp.concatenate([inv, inv])).at[1].set(sign)


def kernel(x, mem, positions, ffn1_norm, ffn1_w_gate, ffn1_w_up, ffn1_w_down, mix_norm, w_in, pool_w, pool_scale, w_out, cross_norm, mem_norm, w_cq, w_ck, w_cv, w_co, ffn2_norm, ffn2_w_gate, ffn2_w_up, ffn2_w_down, final_norm):
    batch, seq, d = x.shape
    assert (seq, d) == (SEQ, D_MODEL) and ffn1_norm.shape[0] == 1
    n_tok = batch * seq
    bf = lambda w: w[0].astype(BF16)
    row = lambda v: v.reshape(1, -1).astype(F32)

    h = x.reshape(n_tok, d)
    pos = positions.reshape(n_tok, 1).astype(jnp.int32)
    fg = row(final_norm)

    h = _ffn(h, row(ffn1_norm), bf(ffn1_w_gate), bf(ffn1_w_up), bf(ffn1_w_down), fg, final_norm=False)
    u_pool, *qkv = _in_proj(h, row(mix_norm), pos, _rope_table(), bf(w_in), batch)
    pool = _pool(u_pool, pool_w[0], row(pool_scale), batch)
    attn = _dil_attn(qkv, batch)
    k_mem, v_mem = _mem_kv(mem.reshape(batch * MEM_LEN, d), row(mem_norm), bf(w_ck), bf(w_cv))
    h = _mix_cross(h, pool, attn, bf(w_out), row(cross_norm), bf(w_cq), k_mem, v_mem, bf(w_co))
    out = _ffn(h, row(ffn2_norm), bf(ffn2_w_gate), bf(ffn2_w_up), bf(ffn2_w_down), fg, final_norm=True)
    return out.reshape(batch, seq, d)
```

```python
import functools

import jax
import jax.numpy as jnp
from jax import lax
from jax.experimental import pallas as pl
from jax.experimental.pallas import tpu as pltpu

F32 = jnp.float32
BF16 = jnp.bfloat16

D_MODEL = 2048
SEQ = 2048
MEM_LEN = 256
HEAD_DIM = 128
POOL_WINDOWS = (2, 4, 8, 16)
POOL_WIDTH = 512
POOL_GROUP = 128
ATTN_WIDTH = 1536
DILATIONS = (1, 4, 16)
HEADS_PER_GROUP = 4
GROUP_WIDTH = HEADS_PER_GROUP * HEAD_DIM
IN_PROJ_WIDTH = 5120
CROSS_HEADS = 4
CROSS_HEAD_DIM = 512
D_FF = 5632
ROPE_THETA = 10000.0
EPS = 1e-6
NEG = -1e30

VMEM_LIMIT_BYTES = 56 * 1024 * 1024
FFN_VMEM_LIMIT_BYTES = 60 * 1024 * 1024

TOKEN_BLOCK = 512
FFN_TOKEN_BLOCK = 1024
CROSS_BLOCK = 256
FF_BLOCK = 512
FF_BLOCK_F32 = 256
BF16_SUBLANES = 16
POOL_PAD = 32
Q_BLOCK = 128
BAND = 64


def _rms(x, g):
    ms = jnp.mean(x * x, axis=-1, keepdims=True)
    return x * lax.rsqrt(ms + EPS) * g


def _silu(x):
    return x / (1.0 + jnp.exp(-x))


def _side_specs(weights, n_steps, step_of):
    blocks, in_specs, out_specs, shapes = [], [], [], []
    for w in weights:
        rows, cols = w.shape
        nb = next(n for n in range(min(n_steps, rows // BF16_SUBLANES), 0, -1)
                  if rows % n == 0 and (rows // n) % BF16_SUBLANES == 0)
        index = lambda *ids, nb=nb: (jnp.minimum(step_of(*ids), nb - 1), 0)
        blocks.append(nb)
        in_specs.append(pl.BlockSpec((rows // nb, cols), index))
        out_specs.append(pl.BlockSpec((rows // nb, cols), index))
        shapes.append(jax.ShapeDtypeStruct((rows, cols), BF16))
    return tuple(blocks), in_specs, out_specs, shapes


def _side_cast(step, side_in, side_out, side_blocks):
    for src, dst, nb in zip(side_in, side_out, side_blocks):
        @pl.when(step < nb)
        def _(src=src, dst=dst):
            dst[...] = src[...].astype(BF16)


def _ffn_kernel(*refs, final_norm, side_blocks):
    n_side = len(side_blocks)
    x_ref, g_ref, wg_ref, wu_ref, wd_ref, fg_ref = refs[:6]
    side_in = refs[6:6 + n_side]
    o_ref = refs[6 + n_side]
    side_out = refs[7 + n_side:7 + 2 * n_side]
    n_ref = refs[7 + 2 * n_side]
    i, j = pl.program_id(0), pl.program_id(1)

    @pl.when(j == 0)
    def _():
        x = x_ref[...]
        n_ref[...] = _rms(x, g_ref[...]).astype(BF16)
        o_ref[...] = x

    _side_cast(i * pl.num_programs(1) + j, side_in, side_out, side_blocks)

    n = n_ref[...]
    gate = jnp.dot(n, wg_ref[...].astype(BF16), preferred_element_type=F32)
    up = jnp.dot(n, wu_ref[...].astype(BF16), preferred_element_type=F32)
    a = (_silu(gate) * (0.5 * up)).astype(BF16)
    o_ref[...] += jnp.dot(a, wd_ref[...].astype(BF16), preferred_element_type=F32)

    if final_norm:
        @pl.when(j == pl.num_programs(1) - 1)
        def _():
            o_ref[...] = _rms(o_ref[...], fg_ref[...])


def _ffn(x, g, wg, wu, wd, fg, side=(), *, final_norm):
    n_tok = x.shape[0]
    ff_block = FF_BLOCK if wg.dtype == BF16 else FF_BLOCK_F32
    grid = (n_tok // FFN_TOKEN_BLOCK, D_FF // ff_block)
    side_blocks, side_in_specs, side_out_specs, side_shapes = _side_specs(
        side, grid[0] * grid[1], lambda i, j: i * grid[1] + j)
    return pl.pallas_call(
        functools.partial(_ffn_kernel, final_norm=final_norm, side_blocks=side_blocks),
        out_shape=[jax.ShapeDtypeStruct((n_tok, D_MODEL), F32)] + side_shapes,
        grid=grid,
        in_specs=[
            pl.BlockSpec((FFN_TOKEN_BLOCK, D_MODEL), lambda i, j: (i, 0)),
            pl.BlockSpec((1, D_MODEL), lambda i, j: (0, 0)),
            pl.BlockSpec((D_MODEL, ff_block), lambda i, j: (0, j)),
            pl.BlockSpec((D_MODEL, ff_block), lambda i, j: (0, j)),
            pl.BlockSpec((ff_block, D_MODEL), lambda i, j: (j, 0)),
            pl.BlockSpec((1, D_MODEL), lambda i, j: (0, 0)),
        ] + side_in_specs,
        out_specs=[pl.BlockSpec((FFN_TOKEN_BLOCK, D_MODEL), lambda i, j: (i, 0))] + side_out_specs,
        scratch_shapes=[pltpu.VMEM((FFN_TOKEN_BLOCK, D_MODEL), BF16)],
        compiler_params=pltpu.CompilerParams(
            dimension_semantics=("arbitrary", "arbitrary"),
            vmem_limit_bytes=FFN_VMEM_LIMIT_BYTES),
        name="ffn_final" if final_norm else "ffn",
    )(x, g, wg, wu, wd, fg, *side)


def _in_proj_kernel(h_ref, g_ref, pos_ref, rope_ref, w_ref,
                    pool_ref, q0_ref, q1_ref, q2_ref, k0_ref, k1_ref, k2_ref, v0_ref, v1_ref, v2_ref,
                    n_ref, z_ref):
    outs = ((q0_ref, q1_ref, q2_ref), (k0_ref, k1_ref, k2_ref), (v0_ref, v1_ref, v2_ref))
    n_ref[...] = _rms(h_ref[...], g_ref[...]).astype(BF16)
    ang = pos_ref[...].astype(F32) * rope_ref[0:1, :]
    cos_k = jnp.cos(ang)
    sin_k = jnp.sin(ang) * rope_ref[1:2, :]
    cos_q = cos_k * (HEAD_DIM ** -0.5)
    sin_q = sin_k * (HEAD_DIM ** -0.5)

    for c in range(IN_PROJ_WIDTH // GROUP_WIDTH):
        z = jnp.dot(n_ref[...], w_ref[:, c * GROUP_WIDTH:(c + 1) * GROUP_WIDTH],
                    preferred_element_type=F32)
        if c == 0:
            pool_ref[...] = z
            continue
        kind, g = divmod(c - 1, len(DILATIONS))
        dil = DILATIONS[g]
        out_ref = outs[kind][g]
        rows = TOKEN_BLOCK // dil
        for hd in range(HEADS_PER_GROUP):
            cols = slice(hd * HEAD_DIM, (hd + 1) * HEAD_DIM)
            t = z[:, cols]
            if kind < 2:
                cos, sin = (cos_q, sin_q) if kind == 0 else (cos_k, sin_k)
                t = t * cos + pltpu.roll(t, HEAD_DIM // 2, 1) * sin
            if dil == 1:
                out_ref[:, cols] = t.astype(BF16)
            else:
                z_ref[hd] = t
                for r in range(dil):
                    out_ref[r, :, cols] = z_ref[hd, pl.ds(r, rows, stride=dil), :].astype(BF16)


def _in_proj(h, g, pos, rope_tab, w_in, batch):
    n_tok = h.shape[0]
    bpb = SEQ // TOKEN_BLOCK
    nat = jax.ShapeDtypeStruct((n_tok, GROUP_WIDTH), BF16)
    nat_spec = pl.BlockSpec((TOKEN_BLOCK, GROUP_WIDTH), lambda i: (i, 0))

    def perm(dil):
        shape = jax.ShapeDtypeStruct((batch, dil, SEQ // dil, GROUP_WIDTH), BF16)
        spec = pl.BlockSpec((None, dil, TOKEN_BLOCK // dil, GROUP_WIDTH),
                            lambda i: (i // bpb, 0, i % bpb, 0))
        return shape, spec

    p1, p2 = perm(DILATIONS[1]), perm(DILATIONS[2])
    qkv_shapes = [nat, p1[0], p2[0]] * 3
    qkv_specs = [nat_spec, p1[1], p2[1]] * 3
    return pl.pallas_call(
        _in_proj_kernel,
        out_shape=[jax.ShapeDtypeStruct((n_tok, POOL_WIDTH), F32)] + qkv_shapes,
        grid=(n_tok // TOKEN_BLOCK,),
        in_specs=[
            pl.BlockSpec((TOKEN_BLOCK, D_MODEL), lambda i: (i, 0)),
            pl.BlockSpec((1, D_MODEL), lambda i: (0, 0)),
            pl.BlockSpec((TOKEN_BLOCK, 1), lambda i: (i, 0)),
            pl.BlockSpec((8, HEAD_DIM), lambda i: (0, 0)),
            pl.BlockSpec((D_MODEL, IN_PROJ_WIDTH), lambda i: (0, 0), pipeline_mode=pl.Buffered(1)),
        ],
        out_specs=[pl.BlockSpec((TOKEN_BLOCK, POOL_WIDTH), lambda i: (i, 0))] + qkv_specs,
        scratch_shapes=[
            pltpu.VMEM((TOKEN_BLOCK, D_MODEL), BF16),
            pltpu.VMEM((HEADS_PER_GROUP, TOKEN_BLOCK, HEAD_DIM), F32),
        ],
        compiler_params=pltpu.CompilerParams(
            dimension_semantics=("parallel",),
            vmem_limit_bytes=VMEM_LIMIT_BYTES),
        name="in_proj",
    )(h, g, pos, rope_tab, w_in)


def _pool_kernel(u_ref, w_ref, s_ref, o_ref, a_ref, b_ref):
    S = SEQ
    P = POOL_PAD
    ext = S + 2 * (P // 2)
    lo = P // 2
    zeros = jnp.zeros((P, POOL_GROUP), F32)
    t = lax.broadcasted_iota(jnp.int32, (S, 1), 0)
    for gi, w in enumerate(POOL_WINDOWS):
        u = u_ref[:, gi * POOL_GROUP:(gi + 1) * POOL_GROUP]
        src, dst = a_ref, b_ref
        src[0:P, :] = zeros
        src[P + S:2 * P + S, :] = zeros
        dst[0:P, :] = zeros
        dst[P + S:2 * P + S, :] = zeros
        src[P:P + S, :] = u
        dst[lo:lo + ext, :] = src[lo - 1:lo - 1 + ext, :] + src[lo:lo + ext, :]
        src, dst = dst, src
        k = 2
        while k < w:
            sh = k // 2
            dst[lo:lo + ext, :] = (src[lo - sh:lo - sh + ext, :] + src[lo + sh:lo + sh + ext, :])
            src, dst = dst, src
            k *= 2
        win = src[P:P + S, :]
        cnt = (jnp.minimum(t + (w - w // 2), S) - jnp.maximum(t - w // 2, 0)).astype(F32)
        pooled = win / cnt - u
        y = jnp.dot(pooled.astype(BF16), w_ref[gi].astype(BF16), preferred_element_type=F32)
        o_ref[:, gi * POOL_GROUP:(gi + 1) * POOL_GROUP] = (
            y * s_ref[:, gi * POOL_GROUP:(gi + 1) * POOL_GROUP]).astype(o_ref.dtype)


def _pool(u_pool, pool_w, pool_scale, batch):
    return pl.pallas_call(
        _pool_kernel,
        out_shape=jax.ShapeDtypeStruct((batch * SEQ, POOL_WIDTH), BF16),
        grid=(batch,),
        in_specs=[
            pl.BlockSpec((SEQ, POOL_WIDTH), lambda b: (b, 0)),
            pl.BlockSpec((len(POOL_WINDOWS), POOL_GROUP, POOL_GROUP), lambda b: (0, 0, 0)),
            pl.BlockSpec((1, POOL_WIDTH), lambda b: (0, 0)),
        ],
        out_specs=pl.BlockSpec((SEQ, POOL_WIDTH), lambda b: (b, 0)),
        scratch_shapes=[
            pltpu.VMEM((SEQ + 2 * POOL_PAD, POOL_GROUP), F32),
            pltpu.VMEM((SEQ + 2 * POOL_PAD, POOL_GROUP), F32),
        ],
        compiler_params=pltpu.CompilerParams(
            dimension_semantics=("parallel",),
            vmem_limit_bytes=VMEM_LIMIT_BYTES),
        name="pool",
    )(u_pool, pool_w, pool_scale)


def _band_block(q, k, v, band):
    nk = k.shape[0]
    s = lax.dot_general(q, k, (((1,), (1,)), ((), ())), preferred_element_type=F32)
    s = jnp.where(band > 0.5, s, NEG)
    m = jnp.max(s, axis=-1, keepdims=True)
    p = jnp.exp(s - m).astype(BF16)
    v_ext = jnp.concatenate([v, jnp.ones((nk, HEAD_DIM), BF16)], axis=1)
    return m, jnp.dot(p, v_ext, preferred_element_type=F32)


def _dil_attn_kernel(q0_ref, q1_ref, q2_ref, k0_ref, k1_ref, k2_ref, v0_ref, v1_ref, v2_ref,
                     o_ref, band_ref, m_ref, l_ref, acc_ref):
    q_refs = (q0_ref, q1_ref, q2_ref)
    k_refs = (k0_ref, k1_ref, k2_ref)
    v_refs = (v0_ref, v1_ref, v2_ref)

    qi = lax.broadcasted_iota(jnp.int32, (Q_BLOCK, 2 * Q_BLOCK), 0)
    kj = lax.broadcasted_iota(jnp.int32, (Q_BLOCK, 2 * Q_BLOCK), 1)
    for n in range(3):
        band_ref[n] = (jnp.abs(kj - qi - n * BAND) <= BAND).astype(F32)

    for g, dil in enumerate(DILATIONS):
        sub_len = SEQ // dil
        nk = min(2 * Q_BLOCK, sub_len)
        for r in range(dil):
            for qb in range(sub_len // Q_BLOCK):
                q_start = qb * Q_BLOCK
                k_start = min(max(q_start - BAND, 0), sub_len - nk)
                if dil == 1:
                    src = lambda ref, start, size: ref[start:start + size, :]
                else:
                    src = lambda ref, start, size: ref[r, start:start + size, :]
                m, pv = _band_block(src(q_refs[g], q_start, Q_BLOCK), src(k_refs[g], k_start, nk),
                                    src(v_refs[g], k_start, nk),
                                    band_ref[(q_start - k_start) // BAND, :, 0:nk])
                rows = pl.ds(r + dil * q_start, Q_BLOCK, stride=dil) if dil > 1 else pl.ds(q_start, Q_BLOCK)
                m_ref[g, rows, :] = jnp.broadcast_to(m, (Q_BLOCK, HEAD_DIM))
                acc_ref[g, rows, :] = pv[:, 0:HEAD_DIM]
                l_ref[g, rows, :] = pv[:, HEAD_DIM:]

    chunk = 256

    def merge(c, carry):
        rows = pl.ds(pl.multiple_of(c * chunk, chunk), chunk)
        m0, m1, m2 = m_ref[0, rows, :], m_ref[1, rows, :], m_ref[2, rows, :]
        mm = jnp.maximum(jnp.maximum(m0, m1), m2)
        w0, w1, w2 = jnp.exp(m0 - mm), jnp.exp(m1 - mm), jnp.exp(m2 - mm)
        num = w0 * acc_ref[0, rows, :] + w1 * acc_ref[1, rows, :] + w2 * acc_ref[2, rows, :]
        den = w0 * l_ref[0, rows, :] + w1 * l_ref[1, rows, :] + w2 * l_ref[2, rows, :]
        o_ref[rows, :] = (num / den).astype(o_ref.dtype)
        return carry

    lax.fori_loop(0, SEQ // chunk, merge, 0)


def _dil_attn(qkv, batch):
    def spec(g):
        dil = DILATIONS[g]
        if dil == 1:
            return pl.BlockSpec((SEQ, HEAD_DIM), lambda b, j: (b, j))
        return pl.BlockSpec((None, dil, SEQ // dil, HEAD_DIM), lambda b, j: (b, 0, 0, j))

    return pl.pallas_call(
        _dil_attn_kernel,
        out_shape=jax.ShapeDtypeStruct((batch * SEQ, GROUP_WIDTH), BF16),
        grid=(batch, HEADS_PER_GROUP),
        in_specs=[spec(g) for g in range(3)] * 3,
        out_specs=pl.BlockSpec((SEQ, HEAD_DIM), lambda b, j: (b, j)),
        scratch_shapes=[
            pltpu.VMEM((3, Q_BLOCK, 2 * Q_BLOCK), F32),
            pltpu.VMEM((3, SEQ, HEAD_DIM), F32),
            pltpu.VMEM((3, SEQ, HEAD_DIM), F32),
            pltpu.VMEM((3, SEQ, HEAD_DIM), F32),
        ],
        compiler_params=pltpu.CompilerParams(
            dimension_semantics=("parallel", "parallel"),
            vmem_limit_bytes=VMEM_LIMIT_BYTES),
        name="dil_attn",
    )(*qkv)


def _mem_kv_kernel(m_ref, g_ref, wk_ref, wv_ref, k_ref, v_ref):
    n = _rms(m_ref[...], g_ref[...]).astype(BF16)
    k_ref[...] = jnp.dot(n, wk_ref[...], preferred_element_type=F32).astype(k_ref.dtype)
    v_ref[...] = jnp.dot(n, wv_ref[...], preferred_element_type=F32).astype(v_ref.dtype)


def _mem_kv(mem, g, w_ck, w_cv):
    n_mem = mem.shape[0]
    col = 512
    out = jax.ShapeDtypeStruct((n_mem, D_MODEL), BF16)
    return pl.pallas_call(
        _mem_kv_kernel,
        out_shape=(out, out),
        grid=(D_MODEL // col,),
        in_specs=[
            pl.BlockSpec((n_mem, D_MODEL), lambda j: (0, 0)),
            pl.BlockSpec((1, D_MODEL), lambda j: (0, 0)),
            pl.BlockSpec((D_MODEL, col), lambda j: (0, j)),
            pl.BlockSpec((D_MODEL, col), lambda j: (0, j)),
        ],
        out_specs=(pl.BlockSpec((n_mem, col), lambda j: (0, j)),
                   pl.BlockSpec((n_mem, col), lambda j: (0, j))),
        compiler_params=pltpu.CompilerParams(
            dimension_semantics=("parallel",),
            vmem_limit_bytes=VMEM_LIMIT_BYTES),
        name="mem_kv",
    )(mem, g, w_ck, w_cv)


def _mix_cross_kernel(*refs, side_blocks):
    n_side = len(side_blocks)
    h_ref, pool_ref, attn_ref, wo_ref, g_ref, wq_ref, k_ref, v_ref, wc_ref = refs[:9]
    side_in = refs[9:9 + n_side]
    o_ref = refs[9 + n_side]
    side_out = refs[10 + n_side:10 + 2 * n_side]
    ctx_ref = refs[10 + 2 * n_side]
    _side_cast(pl.program_id(0), side_in, side_out, side_blocks)
    h2 = (h_ref[...]
          + jnp.dot(pool_ref[...], wo_ref[0:POOL_WIDTH, :], preferred_element_type=F32)
          + jnp.dot(attn_ref[...], wo_ref[POOL_WIDTH:, :], preferred_element_type=F32))
    u = _rms(h2, g_ref[...]).astype(BF16)
    q = jnp.dot(u, wq_ref[...], preferred_element_type=F32).astype(BF16)
    for hd in range(CROSS_HEADS):
        cols = slice(hd * CROSS_HEAD_DIM, (hd + 1) * CROSS_HEAD_DIM)
        s = lax.dot_general(q[:, cols], k_ref[:, cols], (((1,), (1,)), ((), ())),
                            preferred_element_type=F32) * (CROSS_HEAD_DIM ** -0.5)
        m = jnp.max(s, axis=-1, keepdims=True)
        p = jnp.exp(s - m)
        l = jnp.sum(p, axis=-1, keepdims=True)
        ctx = jnp.dot(p.astype(BF16), v_ref[:, cols], preferred_element_type=F32) / l
        ctx_ref[:, cols] = ctx.astype(BF16)
    o_ref[...] = h2 + jnp.dot(ctx_ref[...], wc_ref[...], preferred_element_type=F32)


def _mix_cross(h, pool, attn, w_out, g, w_cq, k_mem, v_mem, w_co, side=()):
    n_tok = h.shape[0]
    blocks_per_batch = SEQ // CROSS_BLOCK
    resident = pl.Buffered(1)
    n_steps = n_tok // CROSS_BLOCK
    side_blocks, side_in_specs, side_out_specs, side_shapes = _side_specs(side, n_steps, lambda i: i)
    return pl.pallas_call(
        functools.partial(_mix_cross_kernel, side_blocks=side_blocks),
        out_shape=[jax.ShapeDtypeStruct((n_tok, D_MODEL), F32)] + side_shapes,
        grid=(n_steps,),
        in_specs=[
            pl.BlockSpec((CROSS_BLOCK, D_MODEL), lambda i: (i, 0)),
            pl.BlockSpec((CROSS_BLOCK, POOL_WIDTH), lambda i: (i, 0)),
            pl.BlockSpec((CROSS_BLOCK, POOL_WIDTH), lambda i: (i, 0)),
            pl.BlockSpec((2 * POOL_WIDTH, D_MODEL), lambda i: (0, 0), pipeline_mode=resident),
            pl.BlockSpec((1, D_MODEL), lambda i: (0, 0)),
            pl.BlockSpec((D_MODEL, D_MODEL), lambda i: (0, 0), pipeline_mode=resident),
            pl.BlockSpec((MEM_LEN, D_MODEL), lambda i: (i // blocks_per_batch, 0)),
            pl.BlockSpec((MEM_LEN, D_MODEL), lambda i: (i // blocks_per_batch, 0)),
            pl.BlockSpec((D_MODEL, D_MODEL), lambda i: (0, 0), pipeline_mode=resident),
        ] + side_in_specs,
        out_specs=[pl.BlockSpec((CROSS_BLOCK, D_MODEL), lambda i: (i, 0))] + side_out_specs,
        scratch_shapes=[pltpu.VMEM((CROSS_BLOCK, D_MODEL), BF16)],
        compiler_params=pltpu.CompilerParams(
            dimension_semantics=("arbitrary",),
            vmem_limit_bytes=VMEM_LIMIT_BYTES),
        name="mix_cross",
    )(h, pool, attn, w_out, g, w_cq, k_mem, v_mem, w_co, *side)


def _rope_table():
    half = HEAD_DIM // 2
    inv = ROPE_THETA ** (-jnp.arange(half, dtype=F32) / half)
    sign = jnp.concatenate([-jnp.ones((half,), F32), jnp.ones((half,), F32)])
    tab = jnp.zeros((8, HEAD_DIM), F32)
    return tab.at[0].set(jnp.concatenate([inv, inv])).at[1].set(sign)


def kernel(x, mem, positions, ffn1_norm, ffn1_w_gate, ffn1_w_up, ffn1_w_down, mix_norm, w_in, pool_w, pool_scale, w_out, cross_norm, mem_norm, w_cq, w_ck, w_cv, w_co, ffn2_norm, ffn2_w_gate, ffn2_w_up, ffn2_w_down, final_norm):
    batch, seq, d = x.shape
    assert (seq, d) == (SEQ, D_MODEL) and ffn1_norm.shape[0] == 1
    n_tok = batch * seq
    row = lambda v: v.reshape(1, -1).astype(F32)

    h = x.reshape(n_tok, d)
    pos = positions.reshape(n_tok, 1).astype(jnp.int32)
    fg = row(final_norm)

    h, w_in_b, w_out_b, w_cq_b, w_ck_b, w_cv_b, w_co_b = _ffn(
        h, row(ffn1_norm), ffn1_w_gate[0], ffn1_w_up[0], ffn1_w_down[0], fg,
        side=(w_in[0], w_out[0], w_cq[0], w_ck[0], w_cv[0], w_co[0]), final_norm=False)
    u_pool, *qkv = _in_proj(h, row(mix_norm), pos, _rope_table(), w_in_b, batch)
    pool = _pool(u_pool, pool_w[0], row(pool_scale), batch)
    attn = _dil_attn(qkv, batch)
    k_mem, v_mem = _mem_kv(mem.reshape(batch * MEM_LEN, d), row(mem_norm), w_ck_b, w_cv_b)
    h, wg2_b, wu2_b, wd2_b = _mix_cross(
        h, pool, attn, w_out_b, row(cross_norm), w_cq_b, k_mem, v_mem, w_co_b,
        side=(ffn2_w_gate[0], ffn2_w_up[0], ffn2_w_down[0]))
    out, = _ffn(h, row(ffn2_norm), wg2_b, wu2_b, wd2_b, fg, final_norm=True)
    return out.reshape(batch, seq, d)
```

```python
import functools

import jax
import jax.numpy as jnp
from jax import lax
from jax.experimental import pallas as pl
from jax.experimental.pallas import tpu as pltpu

F32 = jnp.float32
BF16 = jnp.bfloat16

D_MODEL = 2048
SEQ = 2048
MEM_LEN = 256
HEAD_DIM = 128
POOL_WINDOWS = (2, 4, 8, 16)
POOL_WIDTH = 512
POOL_GROUP = 128
ATTN_WIDTH = 1536
DILATIONS = (1, 4, 16)
HEADS_PER_GROUP = 4
GROUP_WIDTH = HEADS_PER_GROUP * HEAD_DIM
IN_PROJ_WIDTH = 5120
CROSS_HEADS = 4
CROSS_HEAD_DIM = 512
D_FF = 5632
ROPE_THETA = 10000.0
EPS = 1e-6
NEG = -1e30

VMEM_LIMIT_BYTES = 56 * 1024 * 1024
FFN_VMEM_LIMIT_BYTES = 60 * 1024 * 1024

TOKEN_BLOCK = 512
FFN_TOKEN_BLOCK = 1024
CROSS_BLOCK = 256
FF_BLOCK = 512
FF_BLOCK_F32 = 256
BF16_SUBLANES = 16
POOL_PAD = 32
Q_BLOCK = 128
BAND = 64


def _rms(x, g):
    ms = jnp.mean(x * x, axis=-1, keepdims=True)
    return x * lax.rsqrt(ms + EPS) * g


def _silu(x):
    return x / (1.0 + jnp.exp(-x))


def _side_specs(weights, n_steps, step_of):
    blocks, in_specs, out_specs, shapes = [], [], [], []
    for w in weights:
        rows, cols = w.shape
        nb = next(n for n in range(min(n_steps, rows // BF16_SUBLANES), 0, -1)
                  if rows % n == 0 and (rows // n) % BF16_SUBLANES == 0)
        index = lambda *ids, nb=nb: (jnp.minimum(step_of(*ids), nb - 1), 0)
        blocks.append(nb)
        in_specs.append(pl.BlockSpec((rows // nb, cols), index))
        out_specs.append(pl.BlockSpec((rows // nb, cols), index))
        shapes.append(jax.ShapeDtypeStruct((rows, cols), BF16))
    return tuple(blocks), in_specs, out_specs, shapes


def _side_cast(step, side_in, side_out, side_blocks):
    for src, dst, nb in zip(side_in, side_out, side_blocks):
        @pl.when(step < nb)
        def _(src=src, dst=dst):
            dst[...] = src[...].astype(BF16)


def _ffn_kernel(*refs, final_norm, emit_weights, side_blocks):
    n_side = len(side_blocks)
    n_emit = 3 if emit_weights else 0
    x_ref, g_ref, wg_ref, wu_ref, wd_ref, fg_ref = refs[:6]
    side_in = refs[6:6 + n_side]
    o_ref = refs[6 + n_side]
    emit = refs[7 + n_side:7 + n_side + n_emit]
    side_out = refs[7 + n_side + n_emit:7 + 2 * n_side + n_emit]
    n_ref = refs[7 + 2 * n_side + n_emit]
    i, j = pl.program_id(0), pl.program_id(1)

    @pl.when(j == 0)
    def _():
        x = x_ref[...]
        n_ref[...] = _rms(x, g_ref[...]).astype(BF16)
        o_ref[...] = x

    _side_cast(i * pl.num_programs(1) + j, side_in, side_out, side_blocks)

    wg, wu, wd = (r[...].astype(BF16) for r in (wg_ref, wu_ref, wd_ref))
    for dst, w in zip(emit, (wg, wu, wd)):
        dst[...] = w

    n = n_ref[...]
    gate = jnp.dot(n, wg, preferred_element_type=F32)
    up = jnp.dot(n, wu, preferred_element_type=F32)
    a = (_silu(gate) * (0.5 * up)).astype(BF16)
    o_ref[...] += jnp.dot(a, wd, preferred_element_type=F32)

    if final_norm:
        @pl.when(j == pl.num_programs(1) - 1)
        def _():
            o_ref[...] = _rms(o_ref[...], fg_ref[...])


def _ffn(x, g, wg, wu, wd, fg, side=(), *, first_block=0, n_blocks=None, emit_weights=False,
         final_norm=False, name="ffn"):
    if n_blocks is None:
        n_blocks = x.shape[0] // FFN_TOKEN_BLOCK - first_block
    assert not emit_weights or n_blocks == 1
    ff_block = FF_BLOCK if wg.dtype == BF16 else FF_BLOCK_F32
    grid = (n_blocks, D_FF // ff_block)
    side_blocks, side_in_specs, side_out_specs, side_shapes = _side_specs(
        side, grid[0] * grid[1], lambda i, j: i * grid[1] + j)
    col_tile = pl.BlockSpec((D_MODEL, ff_block), lambda i, j: (0, j))
    row_tile = pl.BlockSpec((ff_block, D_MODEL), lambda i, j: (j, 0))
    emit_specs = [col_tile, col_tile, row_tile] if emit_weights else []
    emit_shapes = [jax.ShapeDtypeStruct(w.shape, BF16) for w in (wg, wu, wd)] if emit_weights else []
    return pl.pallas_call(
        functools.partial(_ffn_kernel, final_norm=final_norm, emit_weights=emit_weights,
                          side_blocks=side_blocks),
        out_shape=[jax.ShapeDtypeStruct((n_blocks * FFN_TOKEN_BLOCK, D_MODEL), F32)]
        + emit_shapes + side_shapes,
        grid=grid,
        in_specs=[
            pl.BlockSpec((FFN_TOKEN_BLOCK, D_MODEL), lambda i, j: (i + first_block, 0)),
            pl.BlockSpec((1, D_MODEL), lambda i, j: (0, 0)),
            col_tile, col_tile, row_tile,
            pl.BlockSpec((1, D_MODEL), lambda i, j: (0, 0)),
        ] + side_in_specs,
        out_specs=[pl.BlockSpec((FFN_TOKEN_BLOCK, D_MODEL), lambda i, j: (i, 0))]
        + emit_specs + side_out_specs,
        scratch_shapes=[pltpu.VMEM((FFN_TOKEN_BLOCK, D_MODEL), BF16)],
        compiler_params=pltpu.CompilerParams(
            dimension_semantics=("arbitrary", "arbitrary"),
            vmem_limit_bytes=FFN_VMEM_LIMIT_BYTES),
        name=name,
    )(x, g, wg, wu, wd, fg, *side)


def _split_rows_specs(parts, block):
    n_head = parts[0].shape[0] // block
    return n_head, [pl.BlockSpec((block, D_MODEL), lambda i: (jnp.minimum(i, n_head - 1), 0)),
                    pl.BlockSpec((block, D_MODEL), lambda i: (jnp.maximum(i - n_head, 0), 0))]


def _split_rows_load(n_head, head_ref, tail_ref):
    return jnp.where(pl.program_id(0) < n_head, head_ref[...], tail_ref[...])


def _in_proj_kernel(h_head_ref, h_tail_ref, g_ref, pos_ref, rope_ref, w_ref,
                    pool_ref, q0_ref, q1_ref, q2_ref, k0_ref, k1_ref, k2_ref, v0_ref, v1_ref, v2_ref,
                    n_ref, z_ref, *, n_head):
    outs = ((q0_ref, q1_ref, q2_ref), (k0_ref, k1_ref, k2_ref), (v0_ref, v1_ref, v2_ref))
    h = _split_rows_load(n_head, h_head_ref, h_tail_ref)
    n_ref[...] = _rms(h, g_ref[...]).astype(BF16)
    ang = pos_ref[...].astype(F32) * rope_ref[0:1, :]
    cos_k = jnp.cos(ang)
    sin_k = jnp.sin(ang) * rope_ref[1:2, :]
    cos_q = cos_k * (HEAD_DIM ** -0.5)
    sin_q = sin_k * (HEAD_DIM ** -0.5)

    for c in range(IN_PROJ_WIDTH // GROUP_WIDTH):
        z = jnp.dot(n_ref[...], w_ref[:, c * GROUP_WIDTH:(c + 1) * GROUP_WIDTH],
                    preferred_element_type=F32)
        if c == 0:
            pool_ref[...] = z
            continue
        kind, g = divmod(c - 1, len(DILATIONS))
        dil = DILATIONS[g]
        out_ref = outs[kind][g]
        rows = TOKEN_BLOCK // dil
        for hd in range(HEADS_PER_GROUP):
            cols = slice(hd * HEAD_DIM, (hd + 1) * HEAD_DIM)
            t = z[:, cols]
            if kind < 2:
                cos, sin = (cos_q, sin_q) if kind == 0 else (cos_k, sin_k)
                t = t * cos + pltpu.roll(t, HEAD_DIM // 2, 1) * sin
            if dil == 1:
                out_ref[:, cols] = t.astype(BF16)
            else:
                z_ref[hd] = t
                for r in range(dil):
                    out_ref[r, :, cols] = z_ref[hd, pl.ds(r, rows, stride=dil), :].astype(BF16)


def _in_proj(h_parts, g, pos, rope_tab, w_in, batch):
    n_tok = batch * SEQ
    n_head, h_specs = _split_rows_specs(h_parts, TOKEN_BLOCK)
    bpb = SEQ // TOKEN_BLOCK
    nat = jax.ShapeDtypeStruct((n_tok, GROUP_WIDTH), BF16)
    nat_spec = pl.BlockSpec((TOKEN_BLOCK, GROUP_WIDTH), lambda i: (i, 0))

    def perm(dil):
        shape = jax.ShapeDtypeStruct((batch, dil, SEQ // dil, GROUP_WIDTH), BF16)
        spec = pl.BlockSpec((None, dil, TOKEN_BLOCK // dil, GROUP_WIDTH),
                            lambda i: (i // bpb, 0, i % bpb, 0))
        return shape, spec

    p1, p2 = perm(DILATIONS[1]), perm(DILATIONS[2])
    qkv_shapes = [nat, p1[0], p2[0]] * 3
    qkv_specs = [nat_spec, p1[1], p2[1]] * 3
    return pl.pallas_call(
        functools.partial(_in_proj_kernel, n_head=n_head),
        out_shape=[jax.ShapeDtypeStruct((n_tok, POOL_WIDTH), F32)] + qkv_shapes,
        grid=(n_tok // TOKEN_BLOCK,),
        in_specs=h_specs + [
            pl.BlockSpec((1, D_MODEL), lambda i: (0, 0)),
            pl.BlockSpec((TOKEN_BLOCK, 1), lambda i: (i, 0)),
            pl.BlockSpec((8, HEAD_DIM), lambda i: (0, 0)),
            pl.BlockSpec((D_MODEL, IN_PROJ_WIDTH), lambda i: (0, 0), pipeline_mode=pl.Buffered(1)),
        ],
        out_specs=[pl.BlockSpec((TOKEN_BLOCK, POOL_WIDTH), lambda i: (i, 0))] + qkv_specs,
        scratch_shapes=[
            pltpu.VMEM((TOKEN_BLOCK, D_MODEL), BF16),
            pltpu.VMEM((HEADS_PER_GROUP, TOKEN_BLOCK, HEAD_DIM), F32),
        ],
        compiler_params=pltpu.CompilerParams(
            dimension_semantics=("parallel",),
            vmem_limit_bytes=VMEM_LIMIT_BYTES),
        name="in_proj",
    )(*h_parts, g, pos, rope_tab, w_in)


def _pool_kernel(u_ref, w_ref, s_ref, o_ref, a_ref, b_ref):
    S = SEQ
    P = POOL_PAD
    ext = S + 2 * (P // 2)
    lo = P // 2
    zeros = jnp.zeros((P, POOL_GROUP), F32)
    t = lax.broadcasted_iota(jnp.int32, (S, 1), 0)
    for gi, w in enumerate(POOL_WINDOWS):
        u = u_ref[:, gi * POOL_GROUP:(gi + 1) * POOL_GROUP]
        src, dst = a_ref, b_ref
        src[0:P, :] = zeros
        src[P + S:2 * P + S, :] = zeros
        dst[0:P, :] = zeros
        dst[P + S:2 * P + S, :] = zeros
        src[P:P + S, :] = u
        dst[lo:lo + ext, :] = src[lo - 1:lo - 1 + ext, :] + src[lo:lo + ext, :]
        src, dst = dst, src
        k = 2
        while k < w:
            sh = k // 2
            dst[lo:lo + ext, :] = (src[lo - sh:lo - sh + ext, :] + src[lo + sh:lo + sh + ext, :])
            src, dst = dst, src
            k *= 2
        win = src[P:P + S, :]
        cnt = (jnp.minimum(t + (w - w // 2), S) - jnp.maximum(t - w // 2, 0)).astype(F32)
        pooled = win / cnt - u
        y = jnp.dot(pooled.astype(BF16), w_ref[gi].astype(BF16), preferred_element_type=F32)
        o_ref[:, gi * POOL_GROUP:(gi + 1) * POOL_GROUP] = (
            y * s_ref[:, gi * POOL_GROUP:(gi + 1) * POOL_GROUP]).astype(o_ref.dtype)


def _pool(u_pool, pool_w, pool_scale, batch):
    return pl.pallas_call(
        _pool_kernel,
        out_shape=jax.ShapeDtypeStruct((batch * SEQ, POOL_WIDTH), BF16),
        grid=(batch,),
        in_specs=[
            pl.BlockSpec((SEQ, POOL_WIDTH), lambda b: (b, 0)),
            pl.BlockSpec((len(POOL_WINDOWS), POOL_GROUP, POOL_GROUP), lambda b: (0, 0, 0)),
            pl.BlockSpec((1, POOL_WIDTH), lambda b: (0, 0)),
        ],
        out_specs=pl.BlockSpec((SEQ, POOL_WIDTH), lambda b: (b, 0)),
        scratch_shapes=[
            pltpu.VMEM((SEQ + 2 * POOL_PAD, POOL_GROUP), F32),
            pltpu.VMEM((SEQ + 2 * POOL_PAD, POOL_GROUP), F32),
        ],
        compiler_params=pltpu.CompilerParams(
            dimension_semantics=("parallel",),
            vmem_limit_bytes=VMEM_LIMIT_BYTES),
        name="pool",
    )(u_pool, pool_w, pool_scale)


def _band_block(q, k, v, band):
    nk = k.shape[0]
    s = lax.dot_general(q, k, (((1,), (1,)), ((), ())), preferred_element_type=F32)
    s = jnp.where(band > 0.5, s, NEG)
    m = jnp.max(s, axis=-1, keepdims=True)
    p = jnp.exp(s - m).astype(BF16)
    v_ext = jnp.concatenate([v, jnp.ones((nk, HEAD_DIM), BF16)], axis=1)
    return m, jnp.dot(p, v_ext, preferred_element_type=F32)


def _dil_attn_kernel(q0_ref, q1_ref, q2_ref, k0_ref, k1_ref, k2_ref, v0_ref, v1_ref, v2_ref,
                     o_ref, band_ref, m_ref, l_ref, acc_ref):
    q_refs = (q0_ref, q1_ref, q2_ref)
    k_refs = (k0_ref, k1_ref, k2_ref)
    v_refs = (v0_ref, v1_ref, v2_ref)

    qi = lax.broadcasted_iota(jnp.int32, (Q_BLOCK, 2 * Q_BLOCK), 0)
    kj = lax.broadcasted_iota(jnp.int32, (Q_BLOCK, 2 * Q_BLOCK), 1)
    for n in range(3):
        band_ref[n] = (jnp.abs(kj - qi - n * BAND) <= BAND).astype(F32)

    for g, dil in enumerate(DILATIONS):
        sub_len = SEQ // dil
        nk = min(2 * Q_BLOCK, sub_len)
        for r in range(dil):
            for qb in range(sub_len // Q_BLOCK):
                q_start = qb * Q_BLOCK
                k_start = min(max(q_start - BAND, 0), sub_len - nk)
                if dil == 1:
                    src = lambda ref, start, size: ref[start:start + size, :]
                else:
                    src = lambda ref, start, size: ref[r, start:start + size, :]
                m, pv = _band_block(src(q_refs[g], q_start, Q_BLOCK), src(k_refs[g], k_start, nk),
                                    src(v_refs[g], k_start, nk),
                                    band_ref[(q_start - k_start) // BAND, :, 0:nk])
                rows = pl.ds(r + dil * q_start, Q_BLOCK, stride=dil) if dil > 1 else pl.ds(q_start, Q_BLOCK)
                m_ref[g, rows, :] = jnp.broadcast_to(m, (Q_BLOCK, HEAD_DIM))
                acc_ref[g, rows, :] = pv[:, 0:HEAD_DIM]
                l_ref[g, rows, :] = pv[:, HEAD_DIM:]

    chunk = 256

    def merge(c, carry):
        rows = pl.ds(pl.multiple_of(c * chunk, chunk), chunk)
        m0, m1, m2 = m_ref[0, rows, :], m_ref[1, rows, :], m_ref[2, rows, :]
        mm = jnp.maximum(jnp.maximum(m0, m1), m2)
        w0, w1, w2 = jnp.exp(m0 - mm), jnp.exp(m1 - mm), jnp.exp(m2 - mm)
        num = w0 * acc_ref[0, rows, :] + w1 * acc_ref[1, rows, :] + w2 * acc_ref[2, rows, :]
        den = w0 * l_ref[0, rows, :] + w1 * l_ref[1, rows, :] + w2 * l_ref[2, rows, :]
        o_ref[rows, :] = (num / den).astype(o_ref.dtype)
        return carry

    lax.fori_loop(0, SEQ // chunk, merge, 0)


def _dil_attn(qkv, batch):
    def spec(g):
        dil = DILATIONS[g]
        if dil == 1:
            return pl.BlockSpec((SEQ, HEAD_DIM), lambda b, j: (b, j))
        return pl.BlockSpec((None, dil, SEQ // dil, HEAD_DIM), lambda b, j: (b, 0, 0, j))

    return pl.pallas_call(
        _dil_attn_kernel,
        out_shape=jax.ShapeDtypeStruct((batch * SEQ, GROUP_WIDTH), BF16),
        grid=(batch, HEADS_PER_GROUP),
        in_specs=[spec(g) for g in range(3)] * 3,
        out_specs=pl.BlockSpec((SEQ, HEAD_DIM), lambda b, j: (b, j)),
        scratch_shapes=[
            pltpu.VMEM((3, Q_BLOCK, 2 * Q_BLOCK), F32),
            pltpu.VMEM((3, SEQ, HEAD_DIM), F32),
            pltpu.VMEM((3, SEQ, HEAD_DIM), F32),
            pltpu.VMEM((3, SEQ, HEAD_DIM), F32),
        ],
        compiler_params=pltpu.CompilerParams(
            dimension_semantics=("parallel", "parallel"),
            vmem_limit_bytes=VMEM_LIMIT_BYTES),
        name="dil_attn",
    )(*qkv)


def _mem_kv_kernel(m_ref, g_ref, wk_ref, wv_ref, k_ref, v_ref):
    n = _rms(m_ref[...], g_ref[...]).astype(BF16)
    k_ref[...] = jnp.dot(n, wk_ref[...], preferred_element_type=F32).astype(k_ref.dtype)
    v_ref[...] = jnp.dot(n, wv_ref[...], preferred_element_type=F32).astype(v_ref.dtype)


def _mem_kv(mem, g, w_ck, w_cv):
    n_mem = mem.shape[0]
    col = 512
    out = jax.ShapeDtypeStruct((n_mem, D_MODEL), BF16)
    return pl.pallas_call(
        _mem_kv_kernel,
        out_shape=(out, out),
        grid=(D_MODEL // col,),
        in_specs=[
            pl.BlockSpec((n_mem, D_MODEL), lambda j: (0, 0)),
            pl.BlockSpec((1, D_MODEL), lambda j: (0, 0)),
            pl.BlockSpec((D_MODEL, col), lambda j: (0, j)),
            pl.BlockSpec((D_MODEL, col), lambda j: (0, j)),
        ],
        out_specs=(pl.BlockSpec((n_mem, col), lambda j: (0, j)),
                   pl.BlockSpec((n_mem, col), lambda j: (0, j))),
        compiler_params=pltpu.CompilerParams(
            dimension_semantics=("parallel",),
            vmem_limit_bytes=VMEM_LIMIT_BYTES),
        name="mem_kv",
    )(mem, g, w_ck, w_cv)


def _mix_cross_kernel(*refs, n_head, side_blocks):
    n_side = len(side_blocks)
    (h_head_ref, h_tail_ref, pool_ref, attn_ref, wo_ref, g_ref, wq_ref, k_ref, v_ref,
     wc_ref) = refs[:10]
    side_in = refs[10:10 + n_side]
    o_ref = refs[10 + n_side]
    side_out = refs[11 + n_side:11 + 2 * n_side]
    ctx_ref = refs[11 + 2 * n_side]
    _side_cast(pl.program_id(0), side_in, side_out, side_blocks)
    h2 = (_split_rows_load(n_head, h_head_ref, h_tail_ref)
          + jnp.dot(pool_ref[...], wo_ref[0:POOL_WIDTH, :], preferred_element_type=F32)
          + jnp.dot(attn_ref[...], wo_ref[POOL_WIDTH:, :], preferred_element_type=F32))
    u = _rms(h2, g_ref[...]).astype(BF16)
    q = jnp.dot(u, wq_ref[...], preferred_element_type=F32).astype(BF16)
    for hd in range(CROSS_HEADS):
        cols = slice(hd * CROSS_HEAD_DIM, (hd + 1) * CROSS_HEAD_DIM)
        s = lax.dot_general(q[:, cols], k_ref[:, cols], (((1,), (1,)), ((), ())),
                            preferred_element_type=F32) * (CROSS_HEAD_DIM ** -0.5)
        m = jnp.max(s, axis=-1, keepdims=True)
        p = jnp.exp(s - m)
        l = jnp.sum(p, axis=-1, keepdims=True)
        ctx = jnp.dot(p.astype(BF16), v_ref[:, cols], preferred_element_type=F32) / l
        ctx_ref[:, cols] = ctx.astype(BF16)
    o_ref[...] = h2 + jnp.dot(ctx_ref[...], wc_ref[...], preferred_element_type=F32)


def _mix_cross(h_parts, pool, attn, w_out, g, w_cq, k_mem, v_mem, w_co, side=()):
    n_tok = pool.shape[0]
    n_head, h_specs = _split_rows_specs(h_parts, CROSS_BLOCK)
    blocks_per_batch = SEQ // CROSS_BLOCK
    resident = pl.Buffered(1)
    n_steps = n_tok // CROSS_BLOCK
    side_blocks, side_in_specs, side_out_specs, side_shapes = _side_specs(side, n_steps, lambda i: i)
    return pl.pallas_call(
        functools.partial(_mix_cross_kernel, n_head=n_head, side_blocks=side_blocks),
        out_shape=[jax.ShapeDtypeStruct((n_tok, D_MODEL), F32)] + side_shapes,
        grid=(n_steps,),
        in_specs=h_specs + [
            pl.BlockSpec((CROSS_BLOCK, POOL_WIDTH), lambda i: (i, 0)),
            pl.BlockSpec((CROSS_BLOCK, POOL_WIDTH), lambda i: (i, 0)),
            pl.BlockSpec((2 * POOL_WIDTH, D_MODEL), lambda i: (0, 0), pipeline_mode=resident),
            pl.BlockSpec((1, D_MODEL), lambda i: (0, 0)),
            pl.BlockSpec((D_MODEL, D_MODEL), lambda i: (0, 0), pipeline_mode=resident),
            pl.BlockSpec((MEM_LEN, D_MODEL), lambda i: (i // blocks_per_batch, 0)),
            pl.BlockSpec((MEM_LEN, D_MODEL), lambda i: (i // blocks_per_batch, 0)),
            pl.BlockSpec((D_MODEL, D_MODEL), lambda i: (0, 0), pipeline_mode=resident),
        ] + side_in_specs,
        out_specs=[pl.BlockSpec((CROSS_BLOCK, D_MODEL), lambda i: (i, 0))] + side_out_specs,
        scratch_shapes=[pltpu.VMEM((CROSS_BLOCK, D_MODEL), BF16)],
        compiler_params=pltpu.CompilerParams(
            dimension_semantics=("arbitrary",),
            vmem_limit_bytes=VMEM_LIMIT_BYTES),
        name="mix_cross",
    )(*h_parts, pool, attn, w_out, g, w_cq, k_mem, v_mem, w_co, *side)


def _rope_table():
    half = HEAD_DIM // 2
    inv = ROPE_THETA ** (-jnp.arange(half, dtype=F32) / half)
    sign = jnp.concatenate([-jnp.ones((half,), F32), jnp.ones((half,), F32)])
    tab = jnp.zeros((8, HEAD_DIM), F32)
    return tab.at[0].set(jnp.concatenate([inv, inv])).at[1].set(sign)


def kernel(x, mem, positions, ffn1_norm, ffn1_w_gate, ffn1_w_up, ffn1_w_down, mix_norm, w_in, pool_w, pool_scale, w_out, cross_norm, mem_norm, w_cq, w_ck, w_cv, w_co, ffn2_norm, ffn2_w_gate, ffn2_w_up, ffn2_w_down, final_norm):
    batch, seq, d = x.shape
    assert (seq, d) == (SEQ, D_MODEL) and ffn1_norm.shape[0] == 1
    n_tok = batch * seq
    row = lambda v: v.reshape(1, -1).astype(F32)

    h = x.reshape(n_tok, d)
    pos = positions.reshape(n_tok, 1).astype(jnp.int32)
    fg = row(final_norm)

    h_head, wg1_b, wu1_b, wd1_b = _ffn(
        h, row(ffn1_norm), ffn1_w_gate[0], ffn1_w_up[0], ffn1_w_down[0], fg,
        n_blocks=1, emit_weights=True, name="ffn_head")
    h_tail, w_in_b, w_out_b, w_cq_b, w_ck_b, w_cv_b, w_co_b = _ffn(
        h, row(ffn1_norm), wg1_b, wu1_b, wd1_b, fg,
        side=(w_in[0], w_out[0], w_cq[0], w_ck[0], w_cv[0], w_co[0]), first_block=1, name="ffn")
    h = (h_head, h_tail)
    u_pool, *qkv = _in_proj(h, row(mix_norm), pos, _rope_table(), w_in_b, batch)
    pool = _pool(u_pool, pool_w[0], row(pool_scale), batch)
    attn = _dil_attn(qkv, batch)
    k_mem, v_mem = _mem_kv(mem.reshape(batch * MEM_LEN, d), row(mem_norm), w_ck_b, w_cv_b)
    h, wg2_b, wu2_b, wd2_b = _mix_cross(
        h, pool, attn, w_out_b, row(cross_norm), w_cq_b, k_mem, v_mem, w_co_b,
        side=(ffn2_w_gate[0], ffn2_w_up[0], ffn2_w_down[0]))
    out, = _ffn(h, row(ffn2_norm), wg2_b, wu2_b, wd2_b, fg, final_norm=True, name="ffn_final")
    return out.reshape(batch, seq, d)
```
